```python
import jax, jax.numpy as jnp
from jax import lax
import numpy as np

D_MODEL = 1024
BATCH = 1
SEQ = 16384
DEPTH = 4
DEC_BATCH = 2
DEC_SEQ = 8192
PAST_LEN = 128

N_MIXERS = 2
N_ATTN_LAYERS = (DEPTH + 1) // 2
N_FOURIER_LAYERS = DEPTH // 2
HEAD_DIM = 64
N_Q_HEADS = D_MODEL // HEAD_DIM
N_KV_HEADS = 4
GQA_GROUP = N_Q_HEADS // N_KV_HEADS
QKV_DIM = (N_Q_HEADS + 2 * N_KV_HEADS) * HEAD_DIM
WINDOW = 128
BLOCK = 128
ROPE_THETA = 10000.0
N_FOURIER_GROUPS = 4
FOURIER_GROUP_DIM = D_MODEL // N_FOURIER_GROUPS
D_FF = -(-8 * D_MODEL // (3 * 256)) * 256
EPS = 1e-6
NEG_INF = -1e30

kernel_name = "hybrid_swa_fnet_encoder"


def rms_norm(x, g):
    xf = x.astype(jnp.float32)
    y = xf * lax.rsqrt(jnp.mean(xf * xf, axis=-1, keepdims=True) + EPS)
    return (y * g.astype(jnp.float32)).astype(x.dtype)


def rope(x, pos):
    half = HEAD_DIM // 2
    inv_freq = ROPE_THETA ** (-jnp.arange(half, dtype=jnp.float32) / half)
    ang = pos.astype(jnp.float32)[:, None] * inv_freq[None, :]
    cos = jnp.cos(ang)[None, :, None, :]
    sin = jnp.sin(ang)[None, :, None, :]
    xf = x.astype(jnp.float32)
    x1, x2 = xf[..., :half], xf[..., half:]
    return jnp.concatenate([x1 * cos - x2 * sin, x2 * cos + x1 * sin], axis=-1).astype(x.dtype)


def band_blocks(t):
    b, s, h, d = t.shape
    nb = s // BLOCK
    tp = jnp.pad(t, ((0, 0), (BLOCK, BLOCK), (0, 0), (0, 0))).reshape(b, nb + 2, BLOCK, h, d)
    return jnp.concatenate([tp[:, :-2], tp[:, 1:-1], tp[:, 2:]], axis=2)


def windowed_attention(xn, w_qkv, q_norm_g, k_norm_g, sinks, w_o):
    b, s, _ = xn.shape
    nb = s // BLOCK
    qkv = xn @ w_qkv
    q, k, v = jnp.split(qkv, [N_Q_HEADS * HEAD_DIM, (N_Q_HEADS + N_KV_HEADS) * HEAD_DIM], axis=-1)
    q = q.reshape(b, s, N_Q_HEADS, HEAD_DIM)
    k = k.reshape(b, s, N_KV_HEADS, HEAD_DIM)
    v = v.reshape(b, s, N_KV_HEADS, HEAD_DIM)
    q = rms_norm(q, q_norm_g)
    k = rms_norm(k, k_norm_g)
    pos = jnp.arange(s)
    q = rope(q, pos)
    k = rope(k, pos)
    qb = q.reshape(b, nb, BLOCK, N_KV_HEADS, GQA_GROUP, HEAD_DIM).astype(jnp.float32)
    kw = band_blocks(k).astype(jnp.float32)
    vw = band_blocks(v).astype(jnp.float32)
    scores = jnp.einsum('bnqhgd,bnkhd->bnhgqk', qb, kw) * (HEAD_DIM ** -0.5)
    qi = jnp.arange(BLOCK)[:, None]
    kj = jnp.arange(3 * BLOCK)[None, :]
    rel = kj - BLOCK - qi
    kabs = jnp.arange(nb)[:, None, None] * BLOCK - BLOCK + kj[None]
    mask = (jnp.abs(rel) <= WINDOW)[None] & (kabs >= 0) & (kabs < s)
    scores = jnp.where(mask[None, :, None, None], scores, NEG_INF)
    sink = sinks.astype(jnp.float32).reshape(N_KV_HEADS, GQA_GROUP)[None, None, :, :, None, None]
    m = jnp.maximum(jnp.max(scores, axis=-1, keepdims=True), sink)
    p = jnp.exp(scores - m)
    denom = jnp.sum(p, axis=-1, keepdims=True) + jnp.exp(sink - m)
    out = jnp.einsum('bnhgqk,bnkhd->bnqhgd', p / denom, vw)
    out = out.reshape(b, s, N_Q_HEADS * HEAD_DIM).astype(xn.dtype)
    return out @ w_o


def fourier_mix(xn, w_out):
    b, s, d = xn.shape
    xg = xn.astype(jnp.float32).reshape(b, s, N_FOURIER_GROUPS, FOURIER_GROUP_DIM)
    f = jnp.fft.fft2(xg, axes=(1, 3), norm="ortho").real
    return f.reshape(b, s, d).astype(xn.dtype) @ w_out


def swiglu(xn, w_gate_up, w_down):
    g, u = jnp.split(xn @ w_gate_up, 2, axis=-1)
    return (jax.nn.silu(g) * u) @ w_down


def trunk(x, attn_norm_g, w_qkv, q_norm_g, k_norm_g, attn_sinks, w_o_attn,
          fourier_norm_g, w_fourier_out, ffn_norm_g, w_gate_up, w_down):
    for i in range(DEPTH):
        j = i // N_MIXERS
        if i % N_MIXERS == 0:
            x = x + windowed_attention(rms_norm(x, attn_norm_g[j]), w_qkv[j], q_norm_g[j],
                                       k_norm_g[j], attn_sinks[j], w_o_attn[j])
        else:
            x = x + fourier_mix(rms_norm(x, fourier_norm_g[j]), w_fourier_out[j])
        x = x + swiglu(rms_norm(x, ffn_norm_g[i]), w_gate_up[i], w_down[i])
    return x


def setup_inputs(seed: int = 0) -> dict:
    key = jax.random.key(seed)
    ks = jax.random.split(key, 14)
    f32 = jnp.float32
    nrm = lambda k, shape, scale: jax.random.normal(k, shape, f32) * scale
    return {
        "x_prompt": nrm(ks[0], (BATCH, SEQ, D_MODEL), 1.0),
        "x_sample": nrm(ks[1], (DEC_BATCH, DEC_SEQ, D_MODEL), 1.0),
        "attn_norm_g": 1.0 + nrm(ks[2], (N_ATTN_LAYERS, D_MODEL), 0.02),
        "w_qkv": nrm(ks[3], (N_ATTN_LAYERS, D_MODEL, QKV_DIM), D_MODEL ** -0.5),
        "q_norm_g": 1.0 + nrm(ks[4], (N_ATTN_LAYERS, HEAD_DIM), 0.02),
        "k_norm_g": 1.0 + nrm(ks[5], (N_ATTN_LAYERS, HEAD_DIM), 0.02),
        "attn_sinks": nrm(ks[6], (N_ATTN_LAYERS, N_Q_HEADS), 1.0),
        "w_o_attn": nrm(ks[7], (N_ATTN_LAYERS, N_Q_HEADS * HEAD_DIM, D_MODEL), (N_Q_HEADS * HEAD_DIM) ** -0.5),
        "fourier_norm_g": 1.0 + nrm(ks[8], (N_FOURIER_LAYERS, D_MODEL), 0.02),
        "w_fourier_out": nrm(ks[9], (N_FOURIER_LAYERS, D_MODEL, D_MODEL), D_MODEL ** -0.5),
        "ffn_norm_g": 1.0 + nrm(ks[10], (DEPTH, D_MODEL), 0.02),
        "w_gate_up": nrm(ks[11], (DEPTH, D_MODEL, 2 * D_FF), D_MODEL ** -0.5),
        "w_down": nrm(ks[12], (DEPTH, D_FF, D_MODEL), D_FF ** -0.5),
    }


def reference(x_prompt, x_sample, attn_norm_g, w_qkv, q_norm_g, k_norm_g, attn_sinks, w_o_attn,
              fourier_norm_g, w_fourier_out, ffn_norm_g, w_gate_up, w_down):
    y_prompt = trunk(x_prompt, attn_norm_g, w_qkv, q_norm_g, k_norm_g, attn_sinks, w_o_attn,
                     fourier_norm_g, w_fourier_out, ffn_norm_g, w_gate_up, w_down)
    y_sample = trunk(x_sample, attn_norm_g, w_qkv, q_norm_g, k_norm_g, attn_sinks, w_o_attn,
                     fourier_norm_g, w_fourier_out, ffn_norm_g, w_gate_up, w_down)
    return (y_prompt, y_sample)
```

```python
import functools

import numpy as np
import jax
import jax.numpy as jnp
from jax import lax
from jax.experimental import pallas as pl
from jax.experimental.pallas import tpu as pltpu

D_MODEL = 1024
HEAD_DIM = 64
N_Q_HEADS = 16
N_KV_HEADS = 4
QKV_DIM = (N_Q_HEADS + 2 * N_KV_HEADS) * HEAD_DIM
KV_DIM = N_KV_HEADS * HEAD_DIM
WINDOW = 128
BLOCK = 128
ROPE_THETA = 10000.0
N_FOURIER_GROUPS = 4
FOURIER_GROUP_DIM = D_MODEL // N_FOURIER_GROUPS
D_FF = 2816
EPS = 1e-6
NEG_INF = -1e30
DEPTH = 4

LANES = 128
MXU_DIM = 256
VMEM_LIMIT_BYTES = 56 * 1024 * 1024

ROW_TILE = 512
ATTN_Q_TILE = 512
FFN_CHUNK = 256
DFT_COLS = 4
DFT_ROWS = 4

BF16 = jnp.bfloat16
F32 = jnp.float32


def _resident(shape):
    nd = len(shape)
    return pl.BlockSpec(shape, lambda *_: (0,) * nd, pipeline_mode=pl.Buffered(1))


def _params(n_axes):
    return pltpu.CompilerParams(
        dimension_semantics=("arbitrary",) * n_axes,
        vmem_limit_bytes=VMEM_LIMIT_BYTES,
    )


def _rms_scale(x):
    return lax.rsqrt(jnp.mean(x * x, axis=-1, keepdims=True) + EPS)


def _head_mean_square(t, seg_ref):
    sq = t * t
    hi = sq.astype(BF16)
    lo = (sq - hi.astype(F32)).astype(BF16)
    seg = seg_ref[...]
    cols = []
    for c in range(t.shape[1] // MXU_DIM):
        sl = slice(c * MXU_DIM, (c + 1) * MXU_DIM)
        cols.append(jnp.dot(hi[:, sl], seg, preferred_element_type=F32)
                    + jnp.dot(lo[:, sl], seg, preferred_element_type=F32))
    ss = cols[0] if len(cols) == 1 else jnp.concatenate(cols, axis=1)
    return ss * (1.0 / HEAD_DIM)


def _rope(t, cos, sin_signed):
    half = HEAD_DIM // 2
    lane = lax.broadcasted_iota(jnp.int32, (t.shape[0], LANES), 1)
    first_half = (lane % HEAD_DIM) < half
    outs = []
    for c in range(t.shape[1] // LANES):
        tc = t[:, c * LANES:(c + 1) * LANES]
        fwd = pltpu.roll(tc, LANES - half, axis=1)
        bwd = pltpu.roll(tc, half, axis=1)
        partner = jnp.where(first_half, fwd, bwd)
        outs.append(tc * cos + partner * sin_signed)
    return outs


def _expand_kv(chunks):
    lane = lax.broadcasted_iota(jnp.int32, chunks[0].shape, 1)
    low = lane < HEAD_DIM
    zero = jnp.zeros_like(chunks[0])
    outs = []
    for tc in chunks:
        sw = pltpu.roll(tc, HEAD_DIM, axis=1)
        outs += [jnp.where(low, tc, zero), jnp.where(low, zero, sw),
                 jnp.where(low, sw, zero), jnp.where(low, zero, tc)]
    return jnp.concatenate(outs, axis=1)


def _qkv_kernel(x_ref, g_ref, w_ref, gq_ref, gk_ref, cos_ref, sin_ref, seg_ref,
                q_ref, k_ref, v_ref):
    x = x_ref[...]
    xn = (x * _rms_scale(x) * g_ref[...]).astype(BF16)
    qkv = jnp.dot(xn, w_ref[...], preferred_element_type=F32)
    q = qkv[:, :D_MODEL]
    k = qkv[:, D_MODEL:D_MODEL + KV_DIM]
    v = qkv[:, D_MODEL + KV_DIM:]
    cos = cos_ref[...]
    sin = sin_ref[...]
    qn = q * lax.rsqrt(_head_mean_square(q, seg_ref) + EPS) * gq_ref[...]
    kn = k * lax.rsqrt(_head_mean_square(k, seg_ref) + EPS) * gk_ref[...]
    scale = HEAD_DIM ** -0.5
    q_ref[...] = (jnp.concatenate(_rope(qn, cos, sin), axis=1) * scale).astype(BF16)
    k_ref[...] = _expand_kv(_rope(kn, cos, sin)).astype(BF16)
    v_ref[...] = _expand_kv([v[:, :LANES], v[:, LANES:]]).astype(BF16)


def _qkv_call(x, g, w, gq, gk, cos, sin, seg):
    b, s, _ = x.shape
    tm = min(ROW_TILE, s)
    row = lambda width: pl.BlockSpec((None, tm, width), lambda bi, i: (bi, i, 0))
    tab = pl.BlockSpec((tm, LANES), lambda bi, i: (i, 0))
    out = jax.ShapeDtypeStruct((b, s, D_MODEL), BF16)
    return pl.pallas_call(
        _qkv_kernel,
        out_shape=(out, out, out),
        grid=(b, s // tm),
        in_specs=[row(D_MODEL), _resident((1, D_MODEL)), _resident((D_MODEL, QKV_DIM)),
                  _resident((1, D_MODEL)), _resident((1, KV_DIM)), tab, tab,
                  _resident((MXU_DIM, MXU_DIM))],
        out_specs=(row(D_MODEL), row(D_MODEL), row(D_MODEL)),
        compiler_params=_params(2),
        name="qkv_rope",
    )(x, g, w, gq, gk, cos, sin, seg)


def _attn_kernel(sink_ref, q_ref, kp_ref, kc_ref, kn_ref, vp_ref, vc_ref, vn_ref, bias_ref,
                 o_ref, kbuf, vbuf):
    tq = q_ref.shape[0]
    i = pl.program_id(1)
    last = pl.num_programs(1) - 1
    kbuf[0:BLOCK] = kp_ref[...]
    kbuf[BLOCK:BLOCK + tq] = kc_ref[...]
    kbuf[BLOCK + tq:] = kn_ref[...]
    vbuf[0:BLOCK] = vp_ref[...]
    vbuf[BLOCK:BLOCK + tq] = vc_ref[...]
    vbuf[BLOCK + tq:] = vn_ref[...]
    n_sub = tq // BLOCK
    lane = lax.broadcasted_iota(jnp.int32, (BLOCK, LANES), 1)
    low = lane < HEAD_DIM

    def sub_block(j, carry):
        r0 = pl.multiple_of(j * BLOCK, BLOCK)
        qrows = pl.ds(r0, BLOCK)
        krows = pl.ds(r0, 3 * BLOCK)
        bias = jnp.where(jnp.logical_and(i == 0, j == 0), bias_ref[1],
                         jnp.where(jnp.logical_and(i == last, j == n_sub - 1),
                                   bias_ref[2], bias_ref[0]))
        for h in range(N_KV_HEADS):
            c0 = h * 2 * LANES
            q2 = jnp.concatenate([q_ref[qrows, c0:c0 + LANES],
                                  q_ref[qrows, c0 + LANES:c0 + 2 * LANES]], axis=0)
            kab = jnp.concatenate([kbuf[krows, c0:c0 + LANES],
                                   kbuf[krows, c0 + LANES:c0 + 2 * LANES]], axis=0)
            vab = jnp.concatenate([vbuf[krows, c0:c0 + LANES],
                                   vbuf[krows, c0 + LANES:c0 + 2 * LANES]], axis=0)
            s = lax.dot_general(q2, kab, (((1,), (1,)), ((), ())), preferred_element_type=F32)
            p_rows = []
            inv = []
            for rh in range(2):
                p_cols = []
                for ch in range(2):
                    sink = sink_ref[4 * h + 2 * rh + ch]
                    sc = s[rh * BLOCK:(rh + 1) * BLOCK,
                           ch * 3 * BLOCK:(ch + 1) * 3 * BLOCK] + bias
                    m = jnp.maximum(jnp.max(sc, axis=-1, keepdims=True), sink)
                    p = jnp.exp(sc - m)
                    denom = jnp.sum(p, axis=-1, keepdims=True) + jnp.exp(sink - m)
                    inv.append(1.0 / denom)
                    p_cols.append(p.astype(BF16))
                p_rows.append(jnp.concatenate(p_cols, axis=1))
            p2 = jnp.concatenate(p_rows, axis=0)
            o2 = jnp.dot(p2, vab, preferred_element_type=F32)
            for rh in range(2):
                scale = jnp.where(low, inv[2 * rh], inv[2 * rh + 1])
                o_ref[qrows, c0 + rh * LANES:c0 + (rh + 1) * LANES] = (
                    o2[rh * BLOCK:(rh + 1) * BLOCK] * scale).astype(BF16)
        return carry

    lax.fori_loop(0, n_sub, sub_block, 0)


def _attn_call(sinks, q, k, v, bias):
    b, s, _ = q.shape
    tq = min(ATTN_Q_TILE, s)
    r = tq // BLOCK
    nb = s // BLOCK
    assert s % tq == 0 and s >= 2 * BLOCK
    cur = pl.BlockSpec((None, tq, D_MODEL), lambda bi, i: (bi, i, 0))
    prev = pl.BlockSpec((None, BLOCK, D_MODEL), lambda bi, i: (bi, jnp.maximum(i * r - 1, 0), 0))
    nxt = pl.BlockSpec((None, BLOCK, D_MODEL),
                       lambda bi, i: (bi, jnp.minimum((i + 1) * r, nb - 1), 0))
    return pl.pallas_call(
        _attn_kernel,
        out_shape=jax.ShapeDtypeStruct((b, s, D_MODEL), BF16),
        grid=(b, s // tq),
        in_specs=[pl.BlockSpec(memory_space=pltpu.SMEM), cur, prev, cur, nxt, prev, cur, nxt,
                  _resident((3, BLOCK, 3 * BLOCK))],
        out_specs=cur,
        scratch_shapes=[pltpu.VMEM((tq + 2 * BLOCK, D_MODEL), BF16),
                        pltpu.VMEM((tq + 2 * BLOCK, D_MODEL), BF16)],
        compiler_params=_params(2),
        name="band_attention",
    )(sinks, q, k, k, k, v, v, v, bias)


def _proj_ffn_kernel(x_ref, mix_ref, wp_ref, g_ref, wgu_ref, wd_ref, o_ref):
    x1 = x_ref[...] + jnp.dot(mix_ref[...], wp_ref[...], preferred_element_type=F32)
    xn = (x1 * _rms_scale(x1) * g_ref[...]).astype(BF16)
    acc = x1
    for c in range(D_FF // FFN_CHUNK):
        lo = c * FFN_CHUNK
        gate = jnp.dot(xn, wgu_ref[:, lo:lo + FFN_CHUNK], preferred_element_type=F32)
        up = jnp.dot(xn, wgu_ref[:, D_FF + lo:D_FF + lo + FFN_CHUNK], preferred_element_type=F32)
        hid = (gate * jax.nn.sigmoid(gate) * up).astype(BF16)
        acc = acc + jnp.dot(hid, wd_ref[lo:lo + FFN_CHUNK, :], preferred_element_type=F32)
    o_ref[...] = acc


def _proj_ffn_call(x, mix, wp, g, wgu, wd):
    t = x.shape[0]
    tm = min(ROW_TILE, t)
    row = pl.BlockSpec((tm, D_MODEL), lambda i: (i, 0))
    return pl.pallas_call(
        _proj_ffn_kernel,
        out_shape=jax.ShapeDtypeStruct((t, D_MODEL), F32),
        grid=(t // tm,),
        in_specs=[row, row, _resident((D_MODEL, D_MODEL)), _resident((1, D_MODEL)),
                  _resident((D_MODEL, 2 * D_FF)), _resident((D_FF, D_MODEL))],
        out_specs=row,
        compiler_params=_params(1),
        name="proj_ffn",
    )(x, mix, wp, g, wgu, wd)


def _dft1_kernel(x_ref, g_ref, wc_ref, m1_ref, y_ref):
    n1 = x_ref.shape[0]
    ncol = x_ref.shape[1] // D_MODEL
    xs = jnp.concatenate([x_ref[:, j * D_MODEL:(j + 1) * D_MODEL] for j in range(ncol)], axis=0)
    xn = (xs * _rms_scale(xs) * g_ref[...]).astype(BF16)
    gd = FOURIER_GROUP_DIM
    z = [jnp.dot(xn[:, gi * gd:(gi + 1) * gd], wc_ref[...], preferred_element_type=F32)
         for gi in range(N_FOURIER_GROUPS)]
    zr = jnp.concatenate([zg[:, :gd] for zg in z], axis=1).astype(BF16)
    zi = jnp.concatenate([zg[:, gd:] for zg in z], axis=1).astype(BF16)
    rhs = jnp.concatenate(
        [jnp.concatenate([zr[j * n1:(j + 1) * n1], zi[j * n1:(j + 1) * n1]], axis=0)
         for j in range(ncol)], axis=1)
    y = jnp.dot(m1_ref[...], rhs, preferred_element_type=F32)
    y_ref[0] = y[:n1].astype(BF16)
    y_ref[1] = y[n1:].astype(BF16)


def _dft1_call(x, g, wc, m1):
    b, s, _ = x.shape
    n2 = BLOCK
    n1 = s // n2
    width = DFT_COLS * D_MODEL
    xv = x.reshape(b, n1, n2 * D_MODEL)
    return pl.pallas_call(
        _dft1_kernel,
        out_shape=jax.ShapeDtypeStruct((b, 2, n1, n2 * D_MODEL), BF16),
        grid=(b, n2 // DFT_COLS),
        in_specs=[pl.BlockSpec((None, n1, width), lambda bi, i: (bi, 0, i)),
                  _resident((1, D_MODEL)),
                  _resident((FOURIER_GROUP_DIM, 2 * FOURIER_GROUP_DIM)),
                  _resident((2 * n1, 2 * n1))],
        out_specs=pl.BlockSpec((None, 2, n1, width), lambda bi, i: (bi, 0, 0, i)),
        compiler_params=_params(2),
        name="dft_stage1",
    )(xv, g, wc, m1)


def _dft2_kernel(y_ref, m2_ref, o_ref, *, scale):
    n2 = BLOCK
    for j in range(m2_ref.shape[0]):
        rows = slice(j * n2, (j + 1) * n2)
        rhs = jnp.concatenate([y_ref[0, rows, :], y_ref[1, rows, :]], axis=0)
        out = jnp.dot(m2_ref[j], rhs, preferred_element_type=F32)
        o_ref[:, j * D_MODEL:(j + 1) * D_MODEL] = (out * scale).astype(BF16)


def _dft2_call(y, m2, s):
    b = y.shape[0]
    n2 = BLOCK
    n1 = s // n2
    yv = y.reshape(b, 2, s, D_MODEL)
    scale = float(1.0 / np.sqrt(float(s) * FOURIER_GROUP_DIM))
    out = pl.pallas_call(
        functools.partial(_dft2_kernel, scale=scale),
        out_shape=jax.ShapeDtypeStruct((b, n2, n1 * D_MODEL), BF16),
        grid=(b, n1 // DFT_ROWS),
        in_specs=[pl.BlockSpec((None, 2, DFT_ROWS * n2, D_MODEL), lambda bi, i: (bi, 0, i, 0)),
                  pl.BlockSpec((DFT_ROWS, n2, 2 * n2), lambda bi, i: (i, 0, 0))],
        out_specs=pl.BlockSpec((None, n2, DFT_ROWS * D_MODEL), lambda bi, i: (bi, 0, i)),
        compiler_params=_params(2),
        name="dft_stage2",
    )(yv, m2)
    return out.reshape(b, s, D_MODEL)


def _rope_tables(s):
    half = HEAD_DIM // 2
    inv_freq = ROPE_THETA ** (-jnp.arange(half, dtype=F32) / half)
    ang = jnp.arange(s).astype(F32)[:, None] * inv_freq[None, :]
    cos = jnp.cos(ang)
    sin = jnp.sin(ang)
    reps = LANES // HEAD_DIM
    cos_t = jnp.tile(jnp.concatenate([cos, cos], axis=1), (1, reps))
    sin_t = jnp.tile(jnp.concatenate([-sin, sin], axis=1), (1, reps))
    return cos_t, sin_t


def _band_bias():
    qi = np.arange(BLOCK)[:, None]
    kj = np.arange(3 * BLOCK)[None, :]
    band = np.abs(kj - BLOCK - qi) <= WINDOW
    mid = np.where(band, 0.0, NEG_INF)
    first = np.where(band & (kj >= BLOCK), 0.0, NEG_INF)
    last = np.where(band & (kj < 2 * BLOCK), 0.0, NEG_INF)
    return jnp.asarray(np.stack([mid, first, last]), dtype=F32)


def _segment_ones():
    idx = np.arange(MXU_DIM) // HEAD_DIM
    return jnp.asarray(idx[:, None] == idx[None, :], dtype=BF16)


def _channel_dft():
    c = np.arange(FOURIER_GROUP_DIM)
    ang = 2.0 * np.pi * ((c[:, None] * c[None, :]) % FOURIER_GROUP_DIM) / FOURIER_GROUP_DIM
    return jnp.asarray(np.concatenate([np.cos(ang), -np.sin(ang)], axis=1), dtype=F32).astype(BF16)


def _stage1_dft(n1):
    k = np.arange(n1)
    ang = 2.0 * np.pi * ((k[:, None] * k[None, :]) % n1) / n1
    c, s = np.cos(ang), np.sin(ang)
    return jnp.asarray(np.block([[c, s], [-s, c]]), dtype=F32).astype(BF16)


def _stage2_dft(s):
    n2 = BLOCK
    n1 = s // n2
    k1 = np.arange(n1)[:, None]
    k2 = np.arange(n2)[:, None]
    n = np.arange(n2)[None, :]
    a1 = 2.0 * np.pi * ((k1 * n) % s) / s
    a2 = 2.0 * np.pi * ((k2 * n) % n2) / n2
    c1, s1 = jnp.asarray(np.cos(a1), F32)[:, None, :], jnp.asarray(np.sin(a1), F32)[:, None, :]
    c2, s2 = jnp.asarray(np.cos(a2), F32)[None], jnp.asarray(np.sin(a2), F32)[None]
    cos = c1 * c2 - s1 * s2
    sin = s1 * c2 + c1 * s2
    return jnp.concatenate([cos, sin], axis=2).astype(BF16)


def _trunk(x, p, tables):
    b, s, _ = x.shape
    cos, sin, bias, seg, wc, m1, m2 = tables
    for i in range(DEPTH):
        j = i // 2
        if i % 2 == 0:
            q, k, v = _qkv_call(x, p["attn_norm_g"][j], p["w_qkv"][j], p["q_norm_g"][j],
                                p["k_norm_g"][j], cos, sin, seg)
            mix = _attn_call(p["attn_sinks"][j], q, k, v, bias)
            wp = p["w_o_attn"][j]
        else:
            y = _dft1_call(x, p["fourier_norm_g"][j], wc, m1)
            mix = _dft2_call(y, m2, s)
            wp = p["w_fourier_out"][j]
        x = _proj_ffn_call(x.reshape(b * s, D_MODEL), mix.reshape(b * s, D_MODEL), wp,
                           p["ffn_norm_g"][i], p["w_gate_up"][i], p["w_down"][i]
                           ).reshape(b, s, D_MODEL)
    return x


def kernel(x_prompt, x_sample, attn_norm_g, w_qkv, q_norm_g, k_norm_g, attn_sinks, w_o_attn,
           fourier_norm_g, w_fourier_out, ffn_norm_g, w_gate_up, w_down):
    p = {
        "attn_norm_g": attn_norm_g[:, None, :],
        "w_qkv": w_qkv.astype(BF16),
        "q_norm_g": jnp.tile(q_norm_g, (1, N_Q_HEADS))[:, None, :],
        "k_norm_g": jnp.tile(k_norm_g, (1, N_KV_HEADS))[:, None, :],
        "attn_sinks": attn_sinks,
        "w_o_attn": w_o_attn.astype(BF16),
        "fourier_norm_g": fourier_norm_g[:, None, :],
        "w_fourier_out": w_fourier_out.astype(BF16),
        "ffn_norm_g": ffn_norm_g[:, None, :],
        "w_gate_up": w_gate_up.astype(BF16),
        "w_down": w_down.astype(BF16),
    }
    bias, seg, wc = _band_bias(), _segment_ones(), _channel_dft()
    outs = []
    for x in (x_prompt, x_sample):
        s = x.shape[1]
        cos, sin = _rope_tables(s)
        tables = (cos, sin, bias, seg, wc, _stage1_dft(s // BLOCK), _stage2_dft(s))
        outs.append(_trunk(x, p, tables))
    return tuple(outs)
```

```python
import functools

import numpy as np
import jax
import jax.numpy as jnp
from jax import lax
from jax.experimental import pallas as pl
from jax.experimental.pallas import tpu as pltpu

D_MODEL = 1024
HEAD_DIM = 64
N_Q_HEADS = 16
N_KV_HEADS = 4
QKV_DIM = (N_Q_HEADS + 2 * N_KV_HEADS) * HEAD_DIM
KV_DIM = N_KV_HEADS * HEAD_DIM
WINDOW = 128
BLOCK = 128
ROPE_THETA = 10000.0
N_FOURIER_GROUPS = 4
FOURIER_GROUP_DIM = D_MODEL // N_FOURIER_GROUPS
D_FF = 2816
EPS = 1e-6
NEG_INF = -1e30
DEPTH = 4

LANES = 128
SUBLANES = 8
MXU_DIM = 256
VMEM_LIMIT_BYTES = 56 * 1024 * 1024

ROW_TILE = 512
ATTN_Q_TILE = 512
FFN_CHUNK = 256
DFT_COLS = 4

BF16 = jnp.bfloat16
F32 = jnp.float32


def _resident(shape):
    nd = len(shape)
    return pl.BlockSpec(shape, lambda *_: (0,) * nd, pipeline_mode=pl.Buffered(1))


def _params(n_axes):
    return pltpu.CompilerParams(
        dimension_semantics=("arbitrary",) * n_axes,
        vmem_limit_bytes=VMEM_LIMIT_BYTES,
    )


def _rms_scale(x):
    return lax.rsqrt(jnp.mean(x * x, axis=-1, keepdims=True) + EPS)


def _head_mean_square(t, seg_ref):
    sq = t * t
    hi = sq.astype(BF16)
    lo = (sq - hi.astype(F32)).astype(BF16)
    seg = seg_ref[...]
    cols = []
    for c in range(t.shape[1] // MXU_DIM):
        sl = slice(c * MXU_DIM, (c + 1) * MXU_DIM)
        cols.append(jnp.dot(hi[:, sl], seg, preferred_element_type=F32)
                    + jnp.dot(lo[:, sl], seg, preferred_element_type=F32))
    ss = cols[0] if len(cols) == 1 else jnp.concatenate(cols, axis=1)
    return ss * (1.0 / HEAD_DIM)


def _rope(t, cos, sin_signed):
    half = HEAD_DIM // 2
    lane = lax.broadcasted_iota(jnp.int32, (t.shape[0], LANES), 1)
    first_half = (lane % HEAD_DIM) < half
    outs = []
    for c in range(t.shape[1] // LANES):
        tc = t[:, c * LANES:(c + 1) * LANES]
        fwd = pltpu.roll(tc, LANES - half, axis=1)
        bwd = pltpu.roll(tc, half, axis=1)
        partner = jnp.where(first_half, fwd, bwd)
        outs.append(tc * cos + partner * sin_signed)
    return outs


def _expand_kv(chunks):
    lane = lax.broadcasted_iota(jnp.int32, chunks[0].shape, 1)
    low = lane < HEAD_DIM
    zero = jnp.zeros_like(chunks[0])
    outs = []
    for tc in chunks:
        sw = pltpu.roll(tc, HEAD_DIM, axis=1)
        outs += [jnp.where(low, tc, zero), jnp.where(low, zero, sw),
                 jnp.where(low, sw, zero), jnp.where(low, zero, tc)]
    return jnp.concatenate(outs, axis=1)


def _qkv_kernel(x_ref, g_ref, w_ref, gq_ref, gk_ref, cos_ref, sin_ref, seg_ref,
                q_ref, k_ref, v_ref):
    x = x_ref[...]
    xn = (x * _rms_scale(x) * g_ref[...]).astype(BF16)
    qkv = jnp.dot(xn, w_ref[...], preferred_element_type=F32)
    q = qkv[:, :D_MODEL]
    k = qkv[:, D_MODEL:D_MODEL + KV_DIM]
    v = qkv[:, D_MODEL + KV_DIM:]
    cos = cos_ref[...]
    sin = sin_ref[...]
    qn = q * lax.rsqrt(_head_mean_square(q, seg_ref) + EPS) * gq_ref[...]
    kn = k * lax.rsqrt(_head_mean_square(k, seg_ref) + EPS) * gk_ref[...]
    scale = HEAD_DIM ** -0.5
    q_ref[...] = (jnp.concatenate(_rope(qn, cos, sin), axis=1) * scale).astype(BF16)
    k_ref[...] = _expand_kv(_rope(kn, cos, sin)).astype(BF16)
    v_ref[...] = _expand_kv([v[:, :LANES], v[:, LANES:]]).astype(BF16)


def _qkv_call(x, g, w, gq, gk, cos, sin, seg):
    b, s, _ = x.shape
    tm = min(ROW_TILE, s)
    row = lambda width: pl.BlockSpec((None, tm, width), lambda bi, i: (bi, i, 0))
    tab = pl.BlockSpec((tm, LANES), lambda bi, i: (i, 0))
    out = jax.ShapeDtypeStruct((b, s, D_MODEL), BF16)
    return pl.pallas_call(
        _qkv_kernel,
        out_shape=(out, out, out),
        grid=(b, s // tm),
        in_specs=[row(D_MODEL), _resident((1, D_MODEL)), _resident((D_MODEL, QKV_DIM)),
                  _resident((1, D_MODEL)), _resident((1, KV_DIM)), tab, tab,
                  _resident((MXU_DIM, MXU_DIM))],
        out_specs=(row(D_MODEL), row(D_MODEL), row(D_MODEL)),
        compiler_params=_params(2),
        name="qkv_rope",
    )(x, g, w, gq, gk, cos, sin, seg)


def _attn_kernel(sink_ref, q_ref, kp_ref, kc_ref, kn_ref, vp_ref, vc_ref, vn_ref, bias_ref,
                 o_ref, kbuf, vbuf):
    tq = q_ref.shape[0]
    i = pl.program_id(1)
    last = pl.num_programs(1) - 1
    kbuf[0:BLOCK] = kp_ref[...]
    kbuf[BLOCK:BLOCK + tq] = kc_ref[...]
    kbuf[BLOCK + tq:] = kn_ref[...]
    vbuf[0:BLOCK] = vp_ref[...]
    vbuf[BLOCK:BLOCK + tq] = vc_ref[...]
    vbuf[BLOCK + tq:] = vn_ref[...]
    n_sub = tq // BLOCK
    lane = lax.broadcasted_iota(jnp.int32, (BLOCK, LANES), 1)
    low = lane < HEAD_DIM

    def sub_block(j, carry):
        r0 = pl.multiple_of(j * BLOCK, BLOCK)
        qrows = pl.ds(r0, BLOCK)
        krows = pl.ds(r0, 3 * BLOCK)
        bias = jnp.where(jnp.logical_and(i == 0, j == 0), bias_ref[1],
                         jnp.where(jnp.logical_and(i == last, j == n_sub - 1),
                                   bias_ref[2], bias_ref[0]))
        for h in range(N_KV_HEADS):
            c0 = h * 2 * LANES
            q2 = jnp.concatenate([q_ref[qrows, c0:c0 + LANES],
                                  q_ref[qrows, c0 + LANES:c0 + 2 * LANES]], axis=0)
            kab = jnp.concatenate([kbuf[krows, c0:c0 + LANES],
                                   kbuf[krows, c0 + LANES:c0 + 2 * LANES]], axis=0)
            vab = jnp.concatenate([vbuf[krows, c0:c0 + LANES],
                                   vbuf[krows, c0 + LANES:c0 + 2 * LANES]], axis=0)
            s = lax.dot_general(q2, kab, (((1,), (1,)), ((), ())), preferred_element_type=F32)
            p_rows = []
            inv = []
            for rh in range(2):
                p_cols = []
                for ch in range(2):
                    sink = sink_ref[4 * h + 2 * rh + ch]
                    sc = s[rh * BLOCK:(rh + 1) * BLOCK,
                           ch * 3 * BLOCK:(ch + 1) * 3 * BLOCK] + bias
                    m = jnp.maximum(jnp.max(sc, axis=-1, keepdims=True), sink)
                    p = jnp.exp(sc - m)
                    denom = jnp.sum(p, axis=-1, keepdims=True) + jnp.exp(sink - m)
                    inv.append(1.0 / denom)
                    p_cols.append(p.astype(BF16))
                p_rows.append(jnp.concatenate(p_cols, axis=1))
            p2 = jnp.concatenate(p_rows, axis=0)
            o2 = jnp.dot(p2, vab, preferred_element_type=F32)
            for rh in range(2):
                scale = jnp.where(low, inv[2 * rh], inv[2 * rh + 1])
                o_ref[qrows, c0 + rh * LANES:c0 + (rh + 1) * LANES] = (
                    o2[rh * BLOCK:(rh + 1) * BLOCK] * scale).astype(BF16)
        return carry

    lax.fori_loop(0, n_sub, sub_block, 0)


def _attn_call(sinks, q, k, v, bias):
    b, s, _ = q.shape
    tq = min(ATTN_Q_TILE, s)
    r = tq // BLOCK
    nb = s // BLOCK
    assert s % tq == 0 and s >= 2 * BLOCK
    cur = pl.BlockSpec((None, tq, D_MODEL), lambda bi, i: (bi, i, 0))
    prev = pl.BlockSpec((None, BLOCK, D_MODEL), lambda bi, i: (bi, jnp.maximum(i * r - 1, 0), 0))
    nxt = pl.BlockSpec((None, BLOCK, D_MODEL),
                       lambda bi, i: (bi, jnp.minimum((i + 1) * r, nb - 1), 0))
    return pl.pallas_call(
        _attn_kernel,
        out_shape=jax.ShapeDtypeStruct((b, s, D_MODEL), BF16),
        grid=(b, s // tq),
        in_specs=[pl.BlockSpec(memory_space=pltpu.SMEM), cur, prev, cur, nxt, prev, cur, nxt,
                  _resident((3, BLOCK, 3 * BLOCK))],
        out_specs=cur,
        scratch_shapes=[pltpu.VMEM((tq + 2 * BLOCK, D_MODEL), BF16),
                        pltpu.VMEM((tq + 2 * BLOCK, D_MODEL), BF16)],
        compiler_params=_params(2),
        name="band_attention",
    )(sinks, q, k, k, k, v, v, v, bias)


def _proj_ffn_rows(x, mix, wp_ref, g_ref, wgu_ref, wd_ref):
    x1 = x + jnp.dot(mix, wp_ref[...], preferred_element_type=F32)
    xn = (x1 * _rms_scale(x1) * g_ref[...]).astype(BF16)
    acc = x1
    for c in range(D_FF // FFN_CHUNK):
        lo = c * FFN_CHUNK
        gate = jnp.dot(xn, wgu_ref[:, lo:lo + FFN_CHUNK], preferred_element_type=F32)
        up = jnp.dot(xn, wgu_ref[:, D_FF + lo:D_FF + lo + FFN_CHUNK], preferred_element_type=F32)
        hid = (gate * jax.nn.sigmoid(gate) * up).astype(BF16)
        acc = acc + jnp.dot(hid, wd_ref[lo:lo + FFN_CHUNK, :], preferred_element_type=F32)
    return acc


def _ffn_weight_specs():
    return [_resident((D_MODEL, D_MODEL)), _resident((1, D_MODEL)),
            _resident((D_MODEL, 2 * D_FF)), _resident((D_FF, D_MODEL))]


def _proj_ffn_kernel(x_ref, mix_ref, wp_ref, g_ref, wgu_ref, wd_ref, o_ref):
    o_ref[...] = _proj_ffn_rows(x_ref[...], mix_ref[...], wp_ref, g_ref, wgu_ref, wd_ref)


def _proj_ffn_call(x, mix, wp, g, wgu, wd):
    b, s, _ = x.shape
    t = b * s
    tm = min(ROW_TILE, t)
    row = pl.BlockSpec((tm, D_MODEL), lambda i: (i, 0))
    return pl.pallas_call(
        _proj_ffn_kernel,
        out_shape=jax.ShapeDtypeStruct((t, D_MODEL), F32),
        grid=(t // tm,),
        in_specs=[row, row] + _ffn_weight_specs(),
        out_specs=row,
        compiler_params=_params(1),
        name="proj_ffn",
    )(x.reshape(t, D_MODEL), mix.reshape(t, D_MODEL), wp, g, wgu, wd).reshape(b, s, D_MODEL)


def _proj_ffn_strided_kernel(x_ref, mix_ref, wp_ref, g_ref, wgu_ref, wd_ref, o_ref):
    n2 = x_ref.shape[0]
    half = SUBLANES // 2
    for h in range(2):
        js = range(h * half, (h + 1) * half)
        x = jnp.concatenate([x_ref[:, j, :] for j in js], axis=0)
        mix = jnp.concatenate([mix_ref[j] for j in js], axis=0)
        res = _proj_ffn_rows(x, mix, wp_ref, g_ref, wgu_ref, wd_ref)
        for jj, j in enumerate(js):
            o_ref[:, j, :] = res[jj * n2:(jj + 1) * n2]


def _proj_ffn_strided_call(x, mix_t, wp, g, wgu, wd):
    b, s, _ = x.shape
    n1, n2 = mix_t.shape[1], mix_t.shape[2]
    blk = pl.BlockSpec((None, n2, SUBLANES, D_MODEL), lambda bi, i: (bi, 0, i, 0))
    return pl.pallas_call(
        _proj_ffn_strided_kernel,
        out_shape=jax.ShapeDtypeStruct((b, n2, n1, D_MODEL), F32),
        grid=(b, n1 // SUBLANES),
        in_specs=[blk, pl.BlockSpec((None, SUBLANES, n2, D_MODEL), lambda bi, i: (bi, i, 0, 0))]
        + _ffn_weight_specs(),
        out_specs=blk,
        compiler_params=_params(2),
        name="proj_ffn_strided",
    )(x.reshape(b, n2, n1, D_MODEL), mix_t, wp, g, wgu, wd).reshape(b, s, D_MODEL)


def _dft1_kernel(x_ref, g_ref, wc_ref, m1_ref, y_ref):
    n1 = x_ref.shape[0]
    gd = FOURIER_GROUP_DIM
    for h in range(SUBLANES // DFT_COLS):
        js = range(h * DFT_COLS, (h + 1) * DFT_COLS)
        xs = jnp.concatenate([x_ref[:, j, :] for j in js], axis=0)
        xn = (xs * _rms_scale(xs) * g_ref[...]).astype(BF16)
        z = [jnp.dot(xn[:, gi * gd:(gi + 1) * gd], wc_ref[...], preferred_element_type=F32)
             for gi in range(N_FOURIER_GROUPS)]
        zr = jnp.concatenate([zg[:, :gd] for zg in z], axis=1).astype(BF16)
        zi = jnp.concatenate([zg[:, gd:] for zg in z], axis=1).astype(BF16)
        rhs = jnp.concatenate(
            [jnp.concatenate([zr[jj * n1:(jj + 1) * n1], zi[jj * n1:(jj + 1) * n1]], axis=0)
             for jj in range(DFT_COLS)], axis=1)
        y = jnp.dot(m1_ref[...], rhs, preferred_element_type=F32)
        for jj, j in enumerate(js):
            y_ref[0, j] = y[:n1, jj * D_MODEL:(jj + 1) * D_MODEL]
            y_ref[1, j] = y[n1:, jj * D_MODEL:(jj + 1) * D_MODEL]


def _dft1_call(x, g, wc, m1):
    b, s, _ = x.shape
    n2 = BLOCK
    n1 = s // n2
    return pl.pallas_call(
        _dft1_kernel,
        out_shape=jax.ShapeDtypeStruct((b, 2, n2, n1, D_MODEL), F32),
        grid=(b, n2 // SUBLANES),
        in_specs=[pl.BlockSpec((None, n1, SUBLANES, D_MODEL), lambda bi, i: (bi, 0, i, 0)),
                  _resident((1, D_MODEL)),
                  _resident((FOURIER_GROUP_DIM, 2 * FOURIER_GROUP_DIM)),
                  _resident((2 * n1, 2 * n1))],
        out_specs=pl.BlockSpec((None, 2, SUBLANES, n1, D_MODEL), lambda bi, i: (bi, 0, i, 0, 0)),
        compiler_params=_params(2),
        name="dft_stage1",
    )(x.reshape(b, n1, n2, D_MODEL), g, wc, m1)


def _dft2_kernel(y_ref, m2_ref, o_ref, *, scale):
    for j in range(SUBLANES):
        rhs = jnp.concatenate([y_ref[0, :, j, :], y_ref[1, :, j, :]], axis=0).astype(BF16)
        out = jnp.dot(m2_ref[j], rhs, preferred_element_type=F32)
        o_ref[j] = (out * scale).astype(BF16)


def _dft2_call(y, m2, s):
    b = y.shape[0]
    n2 = BLOCK
    n1 = s // n2
    scale = float(1.0 / np.sqrt(float(s) * FOURIER_GROUP_DIM))
    return pl.pallas_call(
        functools.partial(_dft2_kernel, scale=scale),
        out_shape=jax.ShapeDtypeStruct((b, n1, n2, D_MODEL), BF16),
        grid=(b, n1 // SUBLANES),
        in_specs=[pl.BlockSpec((None, 2, n2, SUBLANES, D_MODEL), lambda bi, i: (bi, 0, 0, i, 0)),
                  pl.BlockSpec((SUBLANES, n2, 2 * n2), lambda bi, i: (i, 0, 0))],
        out_specs=pl.BlockSpec((None, SUBLANES, n2, D_MODEL), lambda bi, i: (bi, i, 0, 0)),
        compiler_params=_params(2),
        name="dft_stage2",
    )(y, m2)


def _rope_tables(s):
    half = HEAD_DIM // 2
    inv_freq = ROPE_THETA ** (-jnp.arange(half, dtype=F32) / half)
    ang = jnp.arange(s).astype(F32)[:, None] * inv_freq[None, :]
    cos = jnp.cos(ang)
    sin = jnp.sin(ang)
    reps = LANES // HEAD_DIM
    cos_t = jnp.tile(jnp.concatenate([cos, cos], axis=1), (1, reps))
    sin_t = jnp.tile(jnp.concatenate([-sin, sin], axis=1), (1, reps))
    return cos_t, sin_t


def _band_bias():
    qi = np.arange(BLOCK)[:, None]
    kj = np.arange(3 * BLOCK)[None, :]
    band = np.abs(kj - BLOCK - qi) <= WINDOW
    mid = np.where(band, 0.0, NEG_INF)
    first = np.where(band & (kj >= BLOCK), 0.0, NEG_INF)
    last = np.where(band & (kj < 2 * BLOCK), 0.0, NEG_INF)
    return jnp.asarray(np.stack([mid, first, last]), dtype=F32)


def _segment_ones():
    idx = np.arange(MXU_DIM) // HEAD_DIM
    return jnp.asarray(idx[:, None] == idx[None, :], dtype=BF16)


def _channel_dft():
    c = np.arange(FOURIER_GROUP_DIM)
    ang = 2.0 * np.pi * ((c[:, None] * c[None, :]) % FOURIER_GROUP_DIM) / FOURIER_GROUP_DIM
    return jnp.asarray(np.concatenate([np.cos(ang), -np.sin(ang)], axis=1), dtype=F32).astype(BF16)


def _stage1_dft(n1):
    k = np.arange(n1)
    ang = 2.0 * np.pi * ((k[:, None] * k[None, :]) % n1) / n1
    c, s = np.cos(ang), np.sin(ang)
    return jnp.asarray(np.block([[c, s], [-s, c]]), dtype=F32).astype(BF16)


def _stage2_dft(s):
    n2 = BLOCK
    n1 = s // n2
    k1 = np.arange(n1)[:, None]
    k2 = np.arange(n2)[:, None]
    n = np.arange(n2)[None, :]
    a1 = 2.0 * np.pi * ((k1 * n) % s) / s
    a2 = 2.0 * np.pi * ((k2 * n) % n2) / n2
    c1, s1 = jnp.asarray(np.cos(a1), F32)[:, None, :], jnp.asarray(np.sin(a1), F32)[:, None, :]
    c2, s2 = jnp.asarray(np.cos(a2), F32)[None], jnp.asarray(np.sin(a2), F32)[None]
    cos = c1 * c2 - s1 * s2
    sin = s1 * c2 + c1 * s2
    return jnp.concatenate([cos, sin], axis=2).astype(BF16)


def _trunk(x, p, tables):
    b, s, _ = x.shape
    cos, sin, bias, seg, wc, m1, m2 = tables
    for i in range(DEPTH):
        j = i // 2
        if i % 2 == 0:
            q, k, v = _qkv_call(x, p["attn_norm_g"][j], p["w_qkv"][j], p["q_norm_g"][j],
                                p["k_norm_g"][j], cos, sin, seg)
            mix = _attn_call(p["attn_sinks"][j], q, k, v, bias)
            x = _proj_ffn_call(x, mix, p["w_o_attn"][j],
                               p["ffn_norm_g"][i], p["w_gate_up"][i], p["w_down"][i])
        else:
            y = _dft1_call(x, p["fourier_norm_g"][j], wc, m1)
            mix_t = _dft2_call(y, m2, s)
            x = _proj_ffn_strided_call(x, mix_t, p["w_fourier_out"][j],
                                       p["ffn_norm_g"][i], p["w_gate_up"][i], p["w_down"][i])
    return x


def kernel(x_prompt, x_sample, attn_norm_g, w_qkv, q_norm_g, k_norm_g, attn_sinks, w_o_attn,
           fourier_norm_g, w_fourier_out, ffn_norm_g, w_gate_up, w_down):
    p = {
        "attn_norm_g": attn_norm_g[:, None, :],
        "w_qkv": w_qkv.astype(BF16),
        "q_norm_g": jnp.tile(q_norm_g, (1, N_Q_HEADS))[:, None, :],
        "k_norm_g": jnp.tile(k_norm_g, (1, N_KV_HEADS))[:, None, :],
        "attn_sinks": attn_sinks,
        "w_o_attn": w_o_attn.astype(BF16),
        "fourier_norm_g": fourier_norm_g[:, None, :],
        "w_fourier_out": w_fourier_out.astype(BF16),
        "ffn_norm_g": ffn_norm_g[:, None, :],
        "w_gate_up": w_gate_up.astype(BF16),
        "w_down": w_down.astype(BF16),
    }
    bias, seg, wc = _band_bias(), _segment_ones(), _channel_dft()
    outs = []
    for x in (x_prompt, x_sample):
        s = x.shape[1]
        cos, sin = _rope_tables(s)
        tables = (cos, sin, bias, seg, wc, _stage1_dft(s // BLOCK), _stage2_dft(s))
        outs.append(_trunk(x, p, tables))
    return tuple(outs)
```

```python
import functools

import numpy as np
import jax
import jax.numpy as jnp
from jax import lax
from jax.experimental import pallas as pl
from jax.experimental.pallas import tpu as pltpu

D_MODEL = 1024
HEAD_DIM = 64
N_Q_HEADS = 16
N_KV_HEADS = 4
QKV_DIM = (N_Q_HEADS + 2 * N_KV_HEADS) * HEAD_DIM
KV_DIM = N_KV_HEADS * HEAD_DIM
WINDOW = 128
BLOCK = 128
ROPE_THETA = 10000.0
N_FOURIER_GROUPS = 4
FOURIER_GROUP_DIM = D_MODEL // N_FOURIER_GROUPS
D_FF = 2816
EPS = 1e-6
NEG_INF = -1e30
DEPTH = 4

LANES = 128
SUBLANES = 8
MXU_DIM = 256
VMEM_LIMIT_BYTES = 56 * 1024 * 1024

ROW_TILE = 512
ATTN_Q_TILE = 512
QKV_SUB_ROWS = 128
FFN_CHUNK = 256
DFT_COLS = 4

LOG2_E = 1.4426950408889634
BF16 = jnp.bfloat16
F32 = jnp.float32


def _resident(shape):
    nd = len(shape)
    return pl.BlockSpec(shape, lambda *_: (0,) * nd, pipeline_mode=pl.Buffered(1))


def _layer(shape, layer):
    nd = len(shape)
    return pl.BlockSpec((None,) + tuple(shape), lambda *_: (layer,) + (0,) * nd,
                        pipeline_mode=pl.Buffered(1))


def _params(n_axes):
    return pltpu.CompilerParams(
        dimension_semantics=("arbitrary",) * n_axes,
        vmem_limit_bytes=VMEM_LIMIT_BYTES,
    )


def _rms_scale(x):
    return lax.rsqrt(jnp.mean(x * x, axis=-1, keepdims=True) + EPS)


def _head_mean_square(t, seg_ref):
    sq = t * t
    hi = sq.astype(BF16)
    lo = (sq - hi.astype(F32)).astype(BF16)
    seg = seg_ref[...]
    cols = []
    for c in range(t.shape[1] // MXU_DIM):
        sl = slice(c * MXU_DIM, (c + 1) * MXU_DIM)
        cols.append(jnp.dot(hi[:, sl], seg, preferred_element_type=F32)
                    + jnp.dot(lo[:, sl], seg, preferred_element_type=F32))
    ss = cols[0] if len(cols) == 1 else jnp.concatenate(cols, axis=1)
    return ss * (1.0 / HEAD_DIM)


def _rope(t, cos, sin_signed):
    half = HEAD_DIM // 2
    lane = lax.broadcasted_iota(jnp.int32, (t.shape[0], LANES), 1)
    first_half = (lane % HEAD_DIM) < half
    outs = []
    for c in range(t.shape[1] // LANES):
        tc = t[:, c * LANES:(c + 1) * LANES]
        fwd = pltpu.roll(tc, LANES - half, axis=1)
        bwd = pltpu.roll(tc, half, axis=1)
        partner = jnp.where(first_half, fwd, bwd)
        outs.append(tc * cos + partner * sin_signed)
    return outs


def _expand_kv(chunks):
    lane = lax.broadcasted_iota(jnp.int32, chunks[0].shape, 1)
    low = lane < HEAD_DIM
    zero = jnp.zeros_like(chunks[0])
    outs = []
    for tc in chunks:
        sw = pltpu.roll(tc, HEAD_DIM, axis=1)
        outs += [jnp.where(low, tc, zero), jnp.where(low, zero, sw),
                 jnp.where(low, sw, zero), jnp.where(low, zero, tc)]
    return jnp.concatenate(outs, axis=1)


def _qkv_kernel(x_ref, g_ref, w_ref, gq_ref, gk_ref, cos_ref, sin_ref, seg_ref,
                q_ref, k_ref, vt_ref):
    scale = HEAD_DIM ** -0.5 * LOG2_E
    for r in range(x_ref.shape[0] // QKV_SUB_ROWS):
        rows = slice(r * QKV_SUB_ROWS, (r + 1) * QKV_SUB_ROWS)
        x = x_ref[rows, :]
        xn = (x * _rms_scale(x) * g_ref[...]).astype(BF16)
        qkv = jnp.dot(xn, w_ref[...], preferred_element_type=F32)
        q = qkv[:, :D_MODEL]
        k = qkv[:, D_MODEL:D_MODEL + KV_DIM]
        v = qkv[:, D_MODEL + KV_DIM:]
        cos = cos_ref[rows, :]
        sin = sin_ref[rows, :]
        qn = q * lax.rsqrt(_head_mean_square(q, seg_ref) + EPS) * gq_ref[...]
        kn = k * lax.rsqrt(_head_mean_square(k, seg_ref) + EPS) * gk_ref[...]
        q_ref[rows, :] = (jnp.concatenate(_rope(qn, cos, sin), axis=1) * scale).astype(BF16)
        k_ref[rows, :] = _expand_kv(_rope(kn, cos, sin)).astype(BF16)
        vt_ref[:, rows] = v.T.astype(BF16)


def _qkv_call(x, layer, g, w, gq, gk, cos, sin, seg):
    b, s, _ = x.shape
    tm = min(ROW_TILE, s)
    row = lambda width: pl.BlockSpec((None, tm, width), lambda bi, i: (bi, i, 0))
    tab = pl.BlockSpec((tm, LANES), lambda bi, i: (i, 0))
    out = jax.ShapeDtypeStruct((b, s, D_MODEL), BF16)
    return pl.pallas_call(
        _qkv_kernel,
        out_shape=(out, out, jax.ShapeDtypeStruct((b, KV_DIM, s), BF16)),
        grid=(b, s // tm),
        in_specs=[row(D_MODEL), _layer((1, D_MODEL), layer), _layer((D_MODEL, QKV_DIM), layer),
                  _layer((1, D_MODEL), layer), _layer((1, KV_DIM), layer), tab, tab,
                  _resident((MXU_DIM, MXU_DIM))],
        out_specs=(row(D_MODEL), row(D_MODEL),
                   pl.BlockSpec((None, KV_DIM, tm), lambda bi, i: (bi, 0, i))),
        compiler_params=_params(2),
        name="qkv_rope",
    )(x, g, w, gq, gk, cos, sin, seg)


def _attn_kernel(sink_ref, q_ref, kp_ref, kc_ref, kn_ref, vp_ref, vc_ref, vn_ref, bias_ref,
                 o_ref, kbuf, vbuf, *, layer):
    tq = q_ref.shape[0]
    i = pl.program_id(1)
    last = pl.num_programs(1) - 1
    kbuf[0:BLOCK] = kp_ref[...]
    kbuf[BLOCK:BLOCK + tq] = kc_ref[...]
    kbuf[BLOCK + tq:] = kn_ref[...]
    vbuf[:, 0:BLOCK] = vp_ref[...]
    vbuf[:, BLOCK:BLOCK + tq] = vc_ref[...]
    vbuf[:, BLOCK + tq:] = vn_ref[...]
    n_sub = tq // BLOCK
    n_stage = n_sub * N_KV_HEADS
    keys = 3 * BLOCK
    row = lax.broadcasted_iota(jnp.int32, (LANES, 2 * BLOCK), 0)
    top = row < HEAD_DIM
    vzero = jnp.zeros((HEAD_DIM, keys), BF16)

    def bias_for(j):
        b0 = bias_ref[0]
        if j == 0:
            b0 = jnp.where(i == 0, bias_ref[1], b0)
        if j == n_sub - 1:
            b0 = jnp.where(i == last, bias_ref[2], b0)
        return b0

    def scores(t):
        j, h = divmod(t, N_KV_HEADS)
        c0 = h * 2 * LANES
        qrows = slice(j * BLOCK, (j + 1) * BLOCK)
        krows = slice(j * BLOCK, (j + 3) * BLOCK)
        q2 = jnp.concatenate([q_ref[qrows, c0:c0 + LANES],
                              q_ref[qrows, c0 + LANES:c0 + 2 * LANES]], axis=0)
        kab = jnp.concatenate([kbuf[krows, c0:c0 + LANES],
                               kbuf[krows, c0 + LANES:c0 + 2 * LANES]], axis=0)
        return lax.dot_general(kab, q2, (((1,), (1,)), ((), ())), preferred_element_type=F32)

    st_next = scores(0)
    bias = None
    for t in range(n_stage):
        j, h = divmod(t, N_KV_HEADS)
        if h == 0:
            bias = bias_for(j)
        c0 = h * 2 * LANES
        qrows = slice(j * BLOCK, (j + 1) * BLOCK)
        st = st_next
        if t + 1 < n_stage:
            st_next = scores(t + 1)
        vt = vbuf[h * HEAD_DIM:(h + 1) * HEAD_DIM, j * BLOCK:(j + 3) * BLOCK]
        vabt = jnp.concatenate([jnp.concatenate([vt, vzero], axis=1),
                                jnp.concatenate([vzero, vt], axis=1)], axis=0)
        p_rows = []
        inv_rows = []
        for ab in range(2):
            p_cols = []
            inv_cols = []
            for ch in range(2):
                sink = sink_ref[layer, 4 * h + 2 * ch + ab] * LOG2_E
                sc = st[ab * keys:(ab + 1) * keys, ch * BLOCK:(ch + 1) * BLOCK]
                sc = jnp.concatenate([sc[:BLOCK] + bias[:BLOCK], sc[BLOCK:2 * BLOCK],
                                      sc[2 * BLOCK:] + bias[2 * BLOCK:]], axis=0)
                m = jnp.maximum(jnp.max(sc, axis=0, keepdims=True), sink)
                p = jnp.exp2(sc - m)
                denom = jnp.sum(p, axis=0, keepdims=True) + jnp.exp2(sink - m)
                inv_cols.append(1.0 / denom)
                p_cols.append(p.astype(BF16))
            p_rows.append(jnp.concatenate(p_cols, axis=1))
            inv_rows.append(jnp.concatenate(inv_cols, axis=1))
        pt = jnp.concatenate(p_rows, axis=0)
        o2t = jnp.dot(vabt, pt, preferred_element_type=F32)
        o2 = (o2t * jnp.where(top, inv_rows[0], inv_rows[1])).T
        o_ref[qrows, c0:c0 + LANES] = o2[:BLOCK].astype(BF16)
        o_ref[qrows, c0 + LANES:c0 + 2 * LANES] = o2[BLOCK:].astype(BF16)


def _attn_call(sinks, layer, q, k, vt, bias_t):
    b, s, _ = q.shape
    tq = min(ATTN_Q_TILE, s)
    r = tq // BLOCK
    nb = s // BLOCK
    assert s % tq == 0 and s >= 2 * BLOCK
    prev_i = lambda i: jnp.maximum(i * r - 1, 0)
    next_i = lambda i: jnp.minimum((i + 1) * r, nb - 1)
    cur = pl.BlockSpec((None, tq, D_MODEL), lambda bi, i: (bi, i, 0))
    prev = pl.BlockSpec((None, BLOCK, D_MODEL), lambda bi, i: (bi, prev_i(i), 0))
    nxt = pl.BlockSpec((None, BLOCK, D_MODEL), lambda bi, i: (bi, next_i(i), 0))
    vcur = pl.BlockSpec((None, KV_DIM, tq), lambda bi, i: (bi, 0, i))
    vprev = pl.BlockSpec((None, KV_DIM, BLOCK), lambda bi, i: (bi, 0, prev_i(i)))
    vnxt = pl.BlockSpec((None, KV_DIM, BLOCK), lambda bi, i: (bi, 0, next_i(i)))
    return pl.pallas_call(
        functools.partial(_attn_kernel, layer=layer),
        out_shape=jax.ShapeDtypeStruct((b, s, D_MODEL), BF16),
        grid=(b, s // tq),
        in_specs=[pl.BlockSpec(memory_space=pltpu.SMEM), cur, prev, cur, nxt, vprev, vcur, vnxt,
                  _resident((3, 3 * BLOCK, BLOCK))],
        out_specs=cur,
        scratch_shapes=[pltpu.VMEM((tq + 2 * BLOCK, D_MODEL), BF16),
                        pltpu.VMEM((KV_DIM, tq + 2 * BLOCK), BF16)],
        compiler_params=_params(2),
        name="band_attention",
    )(sinks, q, k, k, k, vt, vt, vt, bias_t)


def _proj_ffn_rows(x, mix, wp_ref, g_ref, wgu_ref, wd_ref):
    x1 = x + jnp.dot(mix, wp_ref[...], preferred_element_type=F32)
    xn = (x1 * _rms_scale(x1) * g_ref[...]).astype(BF16)
    acc = x1
    for c in range(D_FF // FFN_CHUNK):
        lo = c * FFN_CHUNK
        gate = jnp.dot(xn, wgu_ref[:, lo:lo + FFN_CHUNK], preferred_element_type=F32)
        up = jnp.dot(xn, wgu_ref[:, D_FF + lo:D_FF + lo + FFN_CHUNK], preferred_element_type=F32)
        hid = (gate * jax.nn.sigmoid(gate) * up).astype(BF16)
        acc = acc + jnp.dot(hid, wd_ref[lo:lo + FFN_CHUNK, :], preferred_element_type=F32)
    return acc


def _ffn_weight_specs(mixer_layer, layer):
    return [_layer((D_MODEL, D_MODEL), mixer_layer), _layer((1, D_MODEL), layer),
            _layer((D_MODEL, 2 * D_FF), layer), _layer((D_FF, D_MODEL), layer)]


def _proj_ffn_kernel(x_ref, mix_ref, wp_ref, g_ref, wgu_ref, wd_ref, o_ref):
    o_ref[...] = _proj_ffn_rows(x_ref[...], mix_ref[...], wp_ref, g_ref, wgu_ref, wd_ref)


def _proj_ffn_call(x, mix, mixer_layer, layer, wp, g, wgu, wd):
    b, s, _ = x.shape
    t = b * s
    tm = min(ROW_TILE, t)
    row = pl.BlockSpec((tm, D_MODEL), lambda i: (i, 0))
    return pl.pallas_call(
        _proj_ffn_kernel,
        out_shape=jax.ShapeDtypeStruct((t, D_MODEL), F32),
        grid=(t // tm,),
        in_specs=[row, row] + _ffn_weight_specs(mixer_layer, layer),
        out_specs=row,
        compiler_params=_params(1),
        name="proj_ffn",
    )(x.reshape(t, D_MODEL), mix.reshape(t, D_MODEL), wp, g, wgu, wd).reshape(b, s, D_MODEL)


def _proj_ffn_strided_kernel(x_ref, mix_ref, wp_ref, g_ref, wgu_ref, wd_ref, o_ref):
    n2 = x_ref.shape[0]
    half = SUBLANES // 2
    for h in range(2):
        js = range(h * half, (h + 1) * half)
        x = jnp.concatenate([x_ref[:, j, :] for j in js], axis=0)
        mix = jnp.concatenate([mix_ref[j] for j in js], axis=0)
        res = _proj_ffn_rows(x, mix, wp_ref, g_ref, wgu_ref, wd_ref)
        for jj, j in enumerate(js):
            o_ref[:, j, :] = res[jj * n2:(jj + 1) * n2]


def _proj_ffn_strided_call(x, mix_t, mixer_layer, layer, wp, g, wgu, wd):
    b, s, _ = x.shape
    n1, n2 = mix_t.shape[1], mix_t.shape[2]
    blk = pl.BlockSpec((None, n2, SUBLANES, D_MODEL), lambda bi, i: (bi, 0, i, 0))
    return pl.pallas_call(
        _proj_ffn_strided_kernel,
        out_shape=jax.ShapeDtypeStruct((b, n2, n1, D_MODEL), F32),
        grid=(b, n1 // SUBLANES),
        in_specs=[blk, pl.BlockSpec((None, SUBLANES, n2, D_MODEL), lambda bi, i: (bi, i, 0, 0))]
        + _ffn_weight_specs(mixer_layer, layer),
        out_specs=blk,
        compiler_params=_params(2),
        name="proj_ffn_strided",
    )(x.reshape(b, n2, n1, D_MODEL), mix_t, wp, g, wgu, wd).reshape(b, s, D_MODEL)


def _dft1_kernel(x_ref, g_ref, wc_ref, m1_ref, y_ref):
    n1 = x_ref.shape[0]
    gd = FOURIER_GROUP_DIM
    for h in range(SUBLANES // DFT_COLS):
        js = range(h * DFT_COLS, (h + 1) * DFT_COLS)
        xs = jnp.concatenate([x_ref[:, j, :] for j in js], axis=0)
        xn = (xs * _rms_scale(xs) * g_ref[...]).astype(BF16)
        z = [jnp.dot(xn[:, gi * gd:(gi + 1) * gd], wc_ref[...], preferred_element_type=F32)
             for gi in range(N_FOURIER_GROUPS)]
        zr = jnp.concatenate([zg[:, :gd] for zg in z], axis=1).astype(BF16)
        zi = jnp.concatenate([zg[:, gd:] for zg in z], axis=1).astype(BF16)
        rhs = jnp.concatenate(
            [jnp.concatenate([zr[jj * n1:(jj + 1) * n1], zi[jj * n1:(jj + 1) * n1]], axis=0)
             for jj in range(DFT_COLS)], axis=1)
        y = jnp.dot(m1_ref[...], rhs, preferred_element_type=F32)
        for jj, j in enumerate(js):
            y_ref[0, j] = y[:n1, jj * D_MODEL:(jj + 1) * D_MODEL]
            y_ref[1, j] = y[n1:, jj * D_MODEL:(jj + 1) * D_MODEL]


def _dft1_call(x, layer, g, wc, m1):
    b, s, _ = x.shape
    n2 = BLOCK
    n1 = s // n2
    return pl.pallas_call(
        _dft1_kernel,
        out_shape=jax.ShapeDtypeStruct((b, 2, n2, n1, D_MODEL), F32),
        grid=(b, n2 // SUBLANES),
        in_specs=[pl.BlockSpec((None, n1, SUBLANES, D_MODEL), lambda bi, i: (bi, 0, i, 0)),
                  _layer((1, D_MODEL), layer),
                  _resident((FOURIER_GROUP_DIM, 2 * FOURIER_GROUP_DIM)),
                  _resident((2 * n1, 2 * n1))],
        out_specs=pl.BlockSpec((None, 2, SUBLANES, n1, D_MODEL), lambda bi, i: (bi, 0, i, 0, 0)),
        compiler_params=_params(2),
        name="dft_stage1",
    )(x.reshape(b, n1, n2, D_MODEL), g, wc, m1)


def _dft2_kernel(y_ref, m2_ref, o_ref, *, scale):
    for j in range(SUBLANES):
        rhs = jnp.concatenate([y_ref[0, :, j, :], y_ref[1, :, j, :]], axis=0).astype(BF16)
        out = jnp.dot(m2_ref[j], rhs, preferred_element_type=F32)
        o_ref[j] = (out * scale).astype(BF16)


def _dft2_call(y, m2, s):
    b = y.shape[0]
    n2 = BLOCK
    n1 = s // n2
    scale = float(1.0 / np.sqrt(float(s) * FOURIER_GROUP_DIM))
    return pl.pallas_call(
        functools.partial(_dft2_kernel, scale=scale),
        out_shape=jax.ShapeDtypeStruct((b, n1, n2, D_MODEL), BF16),
        grid=(b, n1 // SUBLANES),
        in_specs=[pl.BlockSpec((None, 2, n2, SUBLANES, D_MODEL), lambda bi, i: (bi, 0, 0, i, 0)),
                  pl.BlockSpec((SUBLANES, n2, 2 * n2), lambda bi, i: (i, 0, 0))],
        out_specs=pl.BlockSpec((None, SUBLANES, n2, D_MODEL), lambda bi, i: (bi, i, 0, 0)),
        compiler_params=_params(2),
        name="dft_stage2",
    )(y, m2)


def _rope_tables(s):
    half = HEAD_DIM // 2
    inv_freq = ROPE_THETA ** (-jnp.arange(half, dtype=F32) / half)
    ang = jnp.arange(s).astype(F32)[:, None] * inv_freq[None, :]
    cos = jnp.cos(ang)
    sin = jnp.sin(ang)
    reps = LANES // HEAD_DIM
    cos_t = jnp.tile(jnp.concatenate([cos, cos], axis=1), (1, reps))
    sin_t = jnp.tile(jnp.concatenate([-sin, sin], axis=1), (1, reps))
    return cos_t, sin_t


def _band_bias():
    qi = np.arange(BLOCK)[:, None]
    kj = np.arange(3 * BLOCK)[None, :]
    band = np.abs(kj - BLOCK - qi) <= WINDOW
    mid = np.where(band, 0.0, NEG_INF)
    first = np.where(band & (kj >= BLOCK), 0.0, NEG_INF)
    last = np.where(band & (kj < 2 * BLOCK), 0.0, NEG_INF)
    return jnp.asarray(np.stack([mid.T, first.T, last.T]), dtype=F32)


def _segment_ones():
    idx = np.arange(MXU_DIM) // HEAD_DIM
    return jnp.asarray(idx[:, None] == idx[None, :], dtype=BF16)


def _channel_dft():
    c = np.arange(FOURIER_GROUP_DIM)
    ang = 2.0 * np.pi * ((c[:, None] * c[None, :]) % FOURIER_GROUP_DIM) / FOURIER_GROUP_DIM
    return jnp.asarray(np.concatenate([np.cos(ang), -np.sin(ang)], axis=1), dtype=F32).astype(BF16)


def _stage1_dft(n1):
    k = np.arange(n1)
    ang = 2.0 * np.pi * ((k[:, None] * k[None, :]) % n1) / n1
    c, s = np.cos(ang), np.sin(ang)
    return jnp.asarray(np.block([[c, s], [-s, c]]), dtype=F32).astype(BF16)


def _stage2_dft(s):
    n2 = BLOCK
    n1 = s // n2
    k1 = np.arange(n1)[:, None]
    k2 = np.arange(n2)[:, None]
    n = np.arange(n2)[None, :]
    a1 = 2.0 * np.pi * ((k1 * n) % s) / s
    a2 = 2.0 * np.pi * ((k2 * n) % n2) / n2
    c1, s1 = jnp.asarray(np.cos(a1), F32)[:, None, :], jnp.asarray(np.sin(a1), F32)[:, None, :]
    c2, s2 = jnp.asarray(np.cos(a2), F32)[None], jnp.asarray(np.sin(a2), F32)[None]
    cos = c1 * c2 - s1 * s2
    sin = s1 * c2 + c1 * s2
    return jnp.concatenate([cos, sin], axis=2).astype(BF16)


def _trunk(x, p, tables):
    b, s, _ = x.shape
    cos, sin, bias, seg, wc, m1, m2 = tables
    for i in range(DEPTH):
        j = i // 2
        if i % 2 == 0:
            q, k, vt = _qkv_call(x, j, p["attn_norm_g"], p["w_qkv"], p["q_norm_g"],
                                 p["k_norm_g"], cos, sin, seg)
            mix = _attn_call(p["attn_sinks"], j, q, k, vt, bias)
            x = _proj_ffn_call(x, mix, j, i, p["w_o_attn"],
                               p["ffn_norm_g"], p["w_gate_up"], p["w_down"])
        else:
            y = _dft1_call(x, j, p["fourier_norm_g"], wc, m1)
            mix_t = _dft2_call(y, m2, s)
            x = _proj_ffn_strided_call(x, mix_t, j, i, p["w_fourier_out"],
                                       p["ffn_norm_g"], p["w_gate_up"], p["w_down"])
    return x


def kernel(x_prompt, x_sample, attn_norm_g, w_qkv, q_norm_g, k_norm_g, attn_sinks, w_o_attn,
           fourier_norm_g, w_fourier_out, ffn_norm_g, w_gate_up, w_down):
    p = {
        "attn_norm_g": attn_norm_g[:, None, :],
        "w_qkv": w_qkv.astype(BF16),
        "q_norm_g": jnp.tile(q_norm_g, (1, N_Q_HEADS))[:, None, :],
        "k_norm_g": jnp.tile(k_norm_g, (1, N_KV_HEADS))[:, None, :],
        "attn_sinks": attn_sinks,
        "w_o_attn": w_o_attn.astype(BF16),
        "fourier_norm_g": fourier_norm_g[:, None, :],
        "w_fourier_out": w_fourier_out.astype(BF16),
        "ffn_norm_g": ffn_norm_g[:, None, :],
        "w_gate_up": w_gate_up.astype(BF16),
        "w_down": w_down.astype(BF16),
    }
    bias, seg, wc = _band_bias(), _segment_ones(), _channel_dft()
    outs = []
    for x in (x_prompt, x_sample):
        s = x.shape[1]
        cos, sin = _rope_tables(s)
        tables = (cos, sin, bias, seg, wc, _stage1_dft(s // BLOCK), _stage2_dft(s))
        outs.append(_trunk(x, p, tables))
    return tuple(outs)
```

```python
import functools

import numpy as np
import jax
import jax.numpy as jnp
from jax import lax
from jax.experimental import pallas as pl
from jax.experimental.pallas import tpu as pltpu

D_MODEL = 1024
HEAD_DIM = 64
N_Q_HEADS = 16
N_KV_HEADS = 4
QKV_DIM = (N_Q_HEADS + 2 * N_KV_HEADS) * HEAD_DIM
KV_DIM = N_KV_HEADS * HEAD_DIM
WINDOW = 128
BLOCK = 128
ROPE_THETA = 10000.0
N_FOURIER_GROUPS = 4
FOURIER_GROUP_DIM = D_MODEL // N_FOURIER_GROUPS
D_FF = 2816
EPS = 1e-6
NEG_INF = -1e30
DEPTH = 4

LANES = 128
SUBLANES = 8
MXU_DIM = 256
VMEM_LIMIT_BYTES = 56 * 1024 * 1024

ROW_TILE = 512
ATTN_Q_TILE = 512
QKV_SUB_ROWS = 128
FFN_CHUNK = 256
DFT_COLS = 4

LOG2_E = 1.4426950408889634
BF16 = jnp.bfloat16
F32 = jnp.float32


def _resident(shape):
    nd = len(shape)
    return pl.BlockSpec(shape, lambda *_: (0,) * nd, pipeline_mode=pl.Buffered(1))


def _layer(shape, layer):
    nd = len(shape)
    return pl.BlockSpec((None,) + tuple(shape), lambda *_: (layer,) + (0,) * nd,
                        pipeline_mode=pl.Buffered(1))


def _params(n_axes, flags=None):
    return pltpu.CompilerParams(
        dimension_semantics=("arbitrary",) * n_axes,
        vmem_limit_bytes=VMEM_LIMIT_BYTES,
        flags=flags,
    )


def _rms_scale(x):
    return lax.rsqrt(jnp.mean(x * x, axis=-1, keepdims=True) + EPS)


def _head_mean_square(t, seg_ref):
    sq = (t * t).astype(BF16)
    seg = seg_ref[...]
    cols = []
    for c in range(t.shape[1] // MXU_DIM):
        sl = slice(c * MXU_DIM, (c + 1) * MXU_DIM)
        cols.append(jnp.dot(sq[:, sl], seg, preferred_element_type=F32))
    ss = cols[0] if len(cols) == 1 else jnp.concatenate(cols, axis=1)
    return ss * (1.0 / HEAD_DIM)


def _rope(t, cos, sin_signed):
    half = HEAD_DIM // 2
    lane = lax.broadcasted_iota(jnp.int32, (t.shape[0], LANES), 1)
    first_half = (lane % HEAD_DIM) < half
    outs = []
    for c in range(t.shape[1] // LANES):
        tc = t[:, c * LANES:(c + 1) * LANES]
        fwd = pltpu.roll(tc, LANES - half, axis=1)
        bwd = pltpu.roll(tc, half, axis=1)
        partner = jnp.where(first_half, fwd, bwd)
        outs.append(tc * cos + partner * sin_signed)
    return outs


def _expand_kv(chunks):
    lane = lax.broadcasted_iota(jnp.int32, chunks[0].shape, 1)
    low = lane < HEAD_DIM
    zero = jnp.zeros_like(chunks[0])
    outs = []
    for tc in chunks:
        sw = pltpu.roll(tc, HEAD_DIM, axis=1)
        outs += [jnp.where(low, tc, zero), jnp.where(low, zero, sw),
                 jnp.where(low, sw, zero), jnp.where(low, zero, tc)]
    return jnp.concatenate(outs, axis=1)


def _qkv_kernel(x_ref, g_ref, w_ref, gq_ref, gk_ref, cos_ref, sin_ref, seg_ref,
                q_ref, k_ref, vt_ref):
    scale = HEAD_DIM ** -0.5 * LOG2_E
    for r in range(x_ref.shape[0] // QKV_SUB_ROWS):
        rows = slice(r * QKV_SUB_ROWS, (r + 1) * QKV_SUB_ROWS)
        x = x_ref[rows, :]
        xn = (x * _rms_scale(x) * g_ref[...]).astype(BF16)
        qkv = jnp.dot(xn, w_ref[...], preferred_element_type=F32)
        q = qkv[:, :D_MODEL]
        k = qkv[:, D_MODEL:D_MODEL + KV_DIM]
        v = qkv[:, D_MODEL + KV_DIM:]
        cos = cos_ref[rows, :]
        sin = sin_ref[rows, :]
        qn = q * lax.rsqrt(_head_mean_square(q, seg_ref) + EPS) * gq_ref[...]
        kn = k * lax.rsqrt(_head_mean_square(k, seg_ref) + EPS) * gk_ref[...]
        q_ref[rows, :] = (jnp.concatenate(_rope(qn, cos, sin), axis=1) * scale).astype(BF16)
        k_ref[rows, :] = _expand_kv(_rope(kn, cos, sin)).astype(BF16)
        vt_ref[:, rows] = v.T.astype(BF16)


def _qkv_call(x, layer, g, w, gq, gk, cos, sin, seg):
    b, s, _ = x.shape
    tm = min(ROW_TILE, s)
    row = lambda width: pl.BlockSpec((None, tm, width), lambda bi, i: (bi, i, 0))
    tab = pl.BlockSpec((tm, LANES), lambda bi, i: (i, 0))
    out = jax.ShapeDtypeStruct((b, s, D_MODEL), BF16)
    return pl.pallas_call(
        _qkv_kernel,
        out_shape=(out, out, jax.ShapeDtypeStruct((b, KV_DIM, s), BF16)),
        grid=(b, s // tm),
        in_specs=[row(D_MODEL), _layer((1, D_MODEL), layer), _layer((D_MODEL, QKV_DIM), layer),
                  _layer((1, D_MODEL), layer), _layer((1, KV_DIM), layer), tab, tab,
                  _resident((MXU_DIM, MXU_DIM))],
        out_specs=(row(D_MODEL), row(D_MODEL),
                   pl.BlockSpec((None, KV_DIM, tm), lambda bi, i: (bi, 0, i))),
        compiler_params=_params(2),
        name="qkv_rope",
    )(x, g, w, gq, gk, cos, sin, seg)


def _attn_kernel(sink_ref, q_ref, kp_ref, kc_ref, kn_ref, vp_ref, vc_ref, vn_ref, bias_ref,
                 o_ref, *, layer):
    tq = q_ref.shape[0]
    i = pl.program_id(1)
    last = pl.num_programs(1) - 1
    n_sub = tq // BLOCK
    n_stage = n_sub * N_KV_HEADS
    keys = 3 * BLOCK
    row = lax.broadcasted_iota(jnp.int32, (LANES, 2 * BLOCK), 0)
    top = row < HEAD_DIM
    vzero = jnp.zeros((HEAD_DIM, keys), BF16)
    ones_row = lax.broadcasted_iota(jnp.int32, (2 * SUBLANES, 2 * keys), 0)
    ones_col = lax.broadcasted_iota(jnp.int32, (2 * SUBLANES, 2 * keys), 1)
    ones_rows = jnp.where((ones_row == 0) & (ones_col < keys) | (ones_row == 1) & (ones_col >= keys),
                          1.0, 0.0).astype(BF16)

    def key_rows(j, lanes):
        parts = []
        if j == 0:
            parts.append(kp_ref[:, lanes])
        parts.append(kc_ref[max(j - 1, 0) * BLOCK:min(j + 2, n_sub) * BLOCK, lanes])
        if j == n_sub - 1:
            parts.append(kn_ref[:, lanes])
        return jnp.concatenate(parts, axis=0)

    def value_cols(j, rows):
        parts = []
        if j == 0:
            parts.append(vp_ref[rows, :])
        parts.append(vc_ref[rows, max(j - 1, 0) * BLOCK:min(j + 2, n_sub) * BLOCK])
        if j == n_sub - 1:
            parts.append(vn_ref[rows, :])
        return jnp.concatenate(parts, axis=1)

    def bias_for(j):
        b0 = bias_ref[0]
        if j == 0:
            b0 = jnp.where(i == 0, bias_ref[1], b0)
        if j == n_sub - 1:
            b0 = jnp.where(i == last, bias_ref[2], b0)
        return b0

    def scores(t):
        j, h = divmod(t, N_KV_HEADS)
        c0 = h * 2 * LANES
        qrows = slice(j * BLOCK, (j + 1) * BLOCK)
        q2 = jnp.concatenate([q_ref[qrows, c0:c0 + LANES],
                              q_ref[qrows, c0 + LANES:c0 + 2 * LANES]], axis=0)
        kab = jnp.concatenate([key_rows(j, slice(c0, c0 + LANES)),
                               key_rows(j, slice(c0 + LANES, c0 + 2 * LANES))], axis=0)
        return lax.dot_general(kab, q2, (((1,), (1,)), ((), ())), preferred_element_type=F32)

    st_next = scores(0)
    bias = None
    for t in range(n_stage):
        j, h = divmod(t, N_KV_HEADS)
        if h == 0:
            bias = bias_for(j)
        c0 = h * 2 * LANES
        qrows = slice(j * BLOCK, (j + 1) * BLOCK)
        st = st_next
        if t + 1 < n_stage:
            st_next = scores(t + 1)
        vt = value_cols(j, slice(h * HEAD_DIM, (h + 1) * HEAD_DIM))
        vabt = jnp.concatenate([jnp.concatenate([vt, vzero], axis=1),
                                jnp.concatenate([vzero, vt], axis=1),
                                ones_rows], axis=0)
        p_rows = []
        e_rows = []
        for ab in range(2):
            p_cols = []
            e_cols = []
            for ch in range(2):
                sink = sink_ref[layer, 4 * h + 2 * ch + ab] * LOG2_E
                sc = st[ab * keys:(ab + 1) * keys, ch * BLOCK:(ch + 1) * BLOCK]
                sc = jnp.concatenate([sc[:BLOCK] + bias[:BLOCK], sc[BLOCK:2 * BLOCK],
                                      sc[2 * BLOCK:] + bias[2 * BLOCK:]], axis=0)
                m = jnp.maximum(jnp.max(sc, axis=0, keepdims=True), sink)
                e_cols.append(jnp.exp2(sink - m))
                p_cols.append(jnp.exp2(sc - m).astype(BF16))
            p_rows.append(jnp.concatenate(p_cols, axis=1))
            e_rows.append(jnp.concatenate(e_cols, axis=1))
        pt = jnp.concatenate(p_rows, axis=0)
        o2t = jnp.dot(vabt, pt, preferred_element_type=F32)
        inv_a = 1.0 / (o2t[LANES:LANES + 1] + e_rows[0])
        inv_b = 1.0 / (o2t[LANES + 1:LANES + 2] + e_rows[1])
        o2 = (o2t[:LANES] * jnp.where(top, inv_a, inv_b)).T
        o_ref[qrows, c0:c0 + LANES] = o2[:BLOCK].astype(BF16)
        o_ref[qrows, c0 + LANES:c0 + 2 * LANES] = o2[BLOCK:].astype(BF16)


def _attn_call(sinks, layer, q, k, vt, bias_t):
    b, s, _ = q.shape
    tq = min(ATTN_Q_TILE, s)
    r = tq // BLOCK
    nb = s // BLOCK
    assert s % tq == 0 and s >= 2 * BLOCK
    prev_i = lambda i: jnp.maximum(i * r - 1, 0)
    next_i = lambda i: jnp.minimum((i + 1) * r, nb - 1)
    cur = pl.BlockSpec((None, tq, D_MODEL), lambda bi, i: (bi, i, 0))
    prev = pl.BlockSpec((None, BLOCK, D_MODEL), lambda bi, i: (bi, prev_i(i), 0))
    nxt = pl.BlockSpec((None, BLOCK, D_MODEL), lambda bi, i: (bi, next_i(i), 0))
    vcur = pl.BlockSpec((None, KV_DIM, tq), lambda bi, i: (bi, 0, i))
    vprev = pl.BlockSpec((None, KV_DIM, BLOCK), lambda bi, i: (bi, 0, prev_i(i)))
    vnxt = pl.BlockSpec((None, KV_DIM, BLOCK), lambda bi, i: (bi, 0, next_i(i)))
    return pl.pallas_call(
        functools.partial(_attn_kernel, layer=layer),
        out_shape=jax.ShapeDtypeStruct((b, s, D_MODEL), BF16),
        grid=(b, s // tq),
        in_specs=[pl.BlockSpec(memory_space=pltpu.SMEM), cur, prev, cur, nxt, vprev, vcur, vnxt,
                  _resident((3, 3 * BLOCK, BLOCK))],
        out_specs=cur,
        compiler_params=_params(2),
        name="band_attention",
    )(sinks, q, k, k, k, vt, vt, vt, bias_t)


def _proj_ffn_rows(x, mix, wp_ref, g_ref, wgu_ref, wd_ref):
    x1 = x + jnp.dot(mix.astype(BF16), wp_ref[...], preferred_element_type=F32)
    xn = (x1 * _rms_scale(x1) * g_ref[...]).astype(BF16)
    acc = x1
    for c in range(D_FF // FFN_CHUNK):
        lo = c * FFN_CHUNK
        gate = jnp.dot(xn, wgu_ref[:, lo:lo + FFN_CHUNK], preferred_element_type=F32)
        up = jnp.dot(xn, wgu_ref[:, D_FF + lo:D_FF + lo + FFN_CHUNK], preferred_element_type=F32)
        hid = (gate * jax.nn.sigmoid(gate) * up).astype(BF16)
        acc = acc + jnp.dot(hid, wd_ref[lo:lo + FFN_CHUNK, :], preferred_element_type=F32)
    return acc


def _ffn_weight_specs(mixer_layer, layer):
    return [_layer((D_MODEL, D_MODEL), mixer_layer), _layer((1, D_MODEL), layer),
            _layer((D_MODEL, 2 * D_FF), layer), _layer((D_FF, D_MODEL), layer)]


def _proj_ffn_kernel(x_ref, mix_ref, wp_ref, g_ref, wgu_ref, wd_ref, o_ref):
    o_ref[...] = _proj_ffn_rows(x_ref[...], mix_ref[...], wp_ref, g_ref, wgu_ref, wd_ref)


def _proj_ffn_call(x, mix, mixer_layer, layer, wp, g, wgu, wd):
    b, s, _ = x.shape
    t = b * s
    tm = min(ROW_TILE, t)
    row = pl.BlockSpec((tm, D_MODEL), lambda i: (i, 0))
    return pl.pallas_call(
        _proj_ffn_kernel,
        out_shape=jax.ShapeDtypeStruct((t, D_MODEL), F32),
        grid=(t // tm,),
        in_specs=[row, row] + _ffn_weight_specs(mixer_layer, layer),
        out_specs=row,
        compiler_params=_params(1),
        name="proj_ffn",
    )(x.reshape(t, D_MODEL), mix.reshape(t, D_MODEL), wp, g, wgu, wd).reshape(b, s, D_MODEL)


def _dft1_kernel(x_ref, g_ref, wc_ref, m1_ref, y_ref):
    n1 = x_ref.shape[0]
    gd = FOURIER_GROUP_DIM
    for h in range(SUBLANES // DFT_COLS):
        js = range(h * DFT_COLS, (h + 1) * DFT_COLS)
        xs = jnp.concatenate([x_ref[:, j, :] for j in js], axis=0)
        xn = (xs * _rms_scale(xs) * g_ref[...]).astype(BF16)
        z = [jnp.dot(xn[:, gi * gd:(gi + 1) * gd], wc_ref[...], preferred_element_type=F32)
             for gi in range(N_FOURIER_GROUPS)]
        zr = jnp.concatenate([zg[:, :gd] for zg in z], axis=1).astype(BF16)
        zi = jnp.concatenate([zg[:, gd:] for zg in z], axis=1).astype(BF16)
        rhs = jnp.concatenate(
            [jnp.concatenate([zr[jj * n1:(jj + 1) * n1], zi[jj * n1:(jj + 1) * n1]], axis=0)
             for jj in range(DFT_COLS)], axis=1)
        y = jnp.dot(m1_ref[...], rhs, preferred_element_type=F32).astype(BF16)
        yw = pltpu.bitcast(y, jnp.uint32)
        for jj, j in enumerate(js):
            y_ref[j] = yw[:, jj * D_MODEL:(jj + 1) * D_MODEL]


def _dft1_call(x, layer, g, wc, m1):
    b, s, _ = x.shape
    n2 = BLOCK
    n1 = s // n2
    return pl.pallas_call(
        _dft1_kernel,
        out_shape=jax.ShapeDtypeStruct((b, n2, n1, D_MODEL), jnp.uint32),
        grid=(b, n2 // SUBLANES),
        in_specs=[pl.BlockSpec((None, n1, SUBLANES, D_MODEL), lambda bi, i: (bi, 0, i, 0)),
                  _layer((1, D_MODEL), layer),
                  _resident((FOURIER_GROUP_DIM, 2 * FOURIER_GROUP_DIM)),
                  _resident((2 * n1, 2 * n1))],
        out_specs=pl.BlockSpec((None, SUBLANES, n1, D_MODEL), lambda bi, i: (bi, i, 0, 0)),
        compiler_params=_params(2),
        name="dft_stage1",
    )(x.reshape(b, n1, n2, D_MODEL), g, wc, m1)


def _dft2_kernel(y_ref, m2_ref, o_ref):
    for j in range(SUBLANES):
        rhs = pltpu.bitcast(y_ref[:, j, :], BF16)
        o_ref[:, j, :] = jnp.dot(m2_ref[j], rhs, preferred_element_type=F32)


def _dft2_call(y, m2, s):
    b = y.shape[0]
    n2 = BLOCK
    n1 = s // n2
    blk = pl.BlockSpec((None, n2, SUBLANES, D_MODEL), lambda bi, i: (bi, 0, i, 0))
    return pl.pallas_call(
        _dft2_kernel,
        out_shape=jax.ShapeDtypeStruct((b, n2, n1, D_MODEL), F32),
        grid=(b, n1 // SUBLANES),
        in_specs=[blk, pl.BlockSpec((SUBLANES, n2, 2 * n2), lambda bi, i: (i, 0, 0))],
        out_specs=blk,
        compiler_params=_params(2),
        name="dft_stage2",
    )(y, m2).reshape(b, s, D_MODEL)


def _rope_tables(s):
    half = HEAD_DIM // 2
    inv_freq = ROPE_THETA ** (-jnp.arange(half, dtype=F32) / half)
    ang = jnp.arange(s).astype(F32)[:, None] * inv_freq[None, :]
    cos = jnp.cos(ang)
    sin = jnp.sin(ang)
    reps = LANES // HEAD_DIM
    cos_t = jnp.tile(jnp.concatenate([cos, cos], axis=1), (1, reps))
    sin_t = jnp.tile(jnp.concatenate([-sin, sin], axis=1), (1, reps))
    return cos_t, sin_t


def _band_bias():
    qi = np.arange(BLOCK)[:, None]
    kj = np.arange(3 * BLOCK)[None, :]
    band = np.abs(kj - BLOCK - qi) <= WINDOW
    mid = np.where(band, 0.0, NEG_INF)
    first = np.where(band & (kj >= BLOCK), 0.0, NEG_INF)
    last = np.where(band & (kj < 2 * BLOCK), 0.0, NEG_INF)
    return jnp.asarray(np.stack([mid.T, first.T, last.T]), dtype=F32)


def _segment_ones():
    idx = np.arange(MXU_DIM) // HEAD_DIM
    return jnp.asarray(idx[:, None] == idx[None, :], dtype=BF16)


def _channel_dft():
    c = np.arange(FOURIER_GROUP_DIM)
    ang = 2.0 * np.pi * ((c[:, None] * c[None, :]) % FOURIER_GROUP_DIM) / FOURIER_GROUP_DIM
    return jnp.asarray(np.concatenate([np.cos(ang), -np.sin(ang)], axis=1), dtype=F32).astype(BF16)


def _stage1_dft(n1):
    k = np.arange(n1)
    ang = 2.0 * np.pi * ((k[:, None] * k[None, :]) % n1) / n1
    c, s = np.cos(ang), np.sin(ang)
    m = np.stack([np.concatenate([c, s], axis=1), np.concatenate([-s, c], axis=1)], axis=1)
    return jnp.asarray(m.reshape(2 * n1, 2 * n1), dtype=F32).astype(BF16)


def _stage2_dft(s):
    n2 = BLOCK
    n1 = s // n2
    k1 = np.arange(n1)[:, None]
    k2 = np.arange(n2)[:, None]
    n = np.arange(n2)[None, :]
    a1 = 2.0 * np.pi * ((k1 * n) % s) / s
    a2 = 2.0 * np.pi * ((k2 * n) % n2) / n2
    c1, s1 = jnp.asarray(np.cos(a1), F32)[:, None, :], jnp.asarray(np.sin(a1), F32)[:, None, :]
    c2, s2 = jnp.asarray(np.cos(a2), F32)[None], jnp.asarray(np.sin(a2), F32)[None]
    scale = float(1.0 / np.sqrt(float(s) * FOURIER_GROUP_DIM))
    cos = (c1 * c2 - s1 * s2) * scale
    sin = (s1 * c2 + c1 * s2) * scale
    return jnp.stack([cos, sin], axis=3).reshape(n1, n2, 2 * n2).astype(BF16)


def _trunk(x, p, tables):
    b, s, _ = x.shape
    cos, sin, bias, seg, wc, m1, m2 = tables
    for i in range(DEPTH):
        j = i // 2
        if i % 2 == 0:
            q, k, vt = _qkv_call(x, j, p["attn_norm_g"], p["w_qkv"], p["q_norm_g"],
                                 p["k_norm_g"], cos, sin, seg)
            mix = _attn_call(p["attn_sinks"], j, q, k, vt, bias)
            x = _proj_ffn_call(x, mix, j, i, p["w_o_attn"],
                               p["ffn_norm_g"], p["w_gate_up"], p["w_down"])
        else:
            y = _dft1_call(x, j, p["fourier_norm_g"], wc, m1)
            mix = _dft2_call(y, m2, s)
            x = _proj_ffn_call(x, mix, j, i, p["w_fourier_out"],
                               p["ffn_norm_g"], p["w_gate_up"], p["w_down"])
    return x


def kernel(x_prompt, x_sample, attn_norm_g, w_qkv, q_norm_g, k_norm_g, attn_sinks, w_o_attn,
           fourier_norm_g, w_fourier_out, ffn_norm_g, w_gate_up, w_down):
    p = {
        "attn_norm_g": attn_norm_g[:, None, :],
        "w_qkv": w_qkv.astype(BF16),
        "q_norm_g": jnp.tile(q_norm_g, (1, N_Q_HEADS))[:, None, :],
        "k_norm_g": jnp.tile(k_norm_g, (1, N_KV_HEADS))[:, None, :],
        "attn_sinks": attn_sinks,
        "w_o_attn": w_o_attn.astype(BF16),
        "fourier_norm_g": fourier_norm_g[:, None, :],
        "w_fourier_out": w_fourier_out.astype(BF16),
        "ffn_norm_g": ffn_norm_g[:, None, :],
        "w_gate_up": w_gate_up.astype(BF16),
        "w_down": w_down.astype(BF16),
    }
    bias, seg, wc = _band_bias(), _segment_ones(), _channel_dft()
    outs = []
    for x in (x_prompt, x_sample):
        s = x.shape[1]
        cos, sin = _rope_tables(s)
        tables = (cos, sin, bias, seg, wc, _stage1_dft(s // BLOCK), _stage2_dft(s))
        outs.append(_trunk(x, p, tables))
    return tuple(outs)
```

```python
import functools

import numpy as np
import jax
import jax.numpy as jnp
from jax import lax
from jax.experimental import pallas as pl
from jax.experimental.pallas import tpu as pltpu

D_MODEL = 1024
HEAD_DIM = 64
N_Q_HEADS = 16
N_KV_HEADS = 4
QKV_DIM = (N_Q_HEADS + 2 * N_KV_HEADS) * HEAD_DIM
KV_DIM = N_KV_HEADS * HEAD_DIM
WINDOW = 128
BLOCK = 128
ROPE_THETA = 10000.0
N_FOURIER_GROUPS = 4
FOURIER_GROUP_DIM = D_MODEL // N_FOURIER_GROUPS
D_FF = 2816
EPS = 1e-6
NEG_INF = -1e30
DEPTH = 4

LANES = 128
SUBLANES = 8
MXU_DIM = 256
VMEM_LIMIT_BYTES = 56 * 1024 * 1024

ROW_TILE = 512
ATTN_Q_TILE = 512
QKV_SUB_ROWS = 128
FFN_CHUNK = 256
DFT_COLS = 4

LOG2_E = 1.4426950408889634
BF16 = jnp.bfloat16
F32 = jnp.float32


def _resident(shape):
    nd = len(shape)
    return pl.BlockSpec(shape, lambda *_: (0,) * nd, pipeline_mode=pl.Buffered(1))


def _layer(shape, layer):
    nd = len(shape)
    return pl.BlockSpec((None,) + tuple(shape), lambda *_: (layer,) + (0,) * nd,
                        pipeline_mode=pl.Buffered(1))


def _params(n_axes, flags=None):
    return pltpu.CompilerParams(
        dimension_semantics=("arbitrary",) * n_axes,
        vmem_limit_bytes=VMEM_LIMIT_BYTES,
        flags=flags,
    )


def _rms_scale(x):
    return lax.rsqrt(jnp.mean(x * x, axis=-1, keepdims=True) + EPS)


def _head_mean_square(t, seg_ref):
    sq = (t * t).astype(BF16)
    seg = seg_ref[...]
    cols = []
    for c in range(t.shape[1] // MXU_DIM):
        sl = slice(c * MXU_DIM, (c + 1) * MXU_DIM)
        cols.append(jnp.dot(sq[:, sl], seg, preferred_element_type=F32))
    ss = cols[0] if len(cols) == 1 else jnp.concatenate(cols, axis=1)
    return ss * (1.0 / HEAD_DIM)


def _rope(t, cos, sin_signed):
    half = HEAD_DIM // 2
    lane = lax.broadcasted_iota(jnp.int32, (t.shape[0], LANES), 1)
    first_half = (lane % HEAD_DIM) < half
    outs = []
    for c in range(t.shape[1] // LANES):
        tc = t[:, c * LANES:(c + 1) * LANES]
        fwd = pltpu.roll(tc, LANES - half, axis=1)
        bwd = pltpu.roll(tc, half, axis=1)
        partner = jnp.where(first_half, fwd, bwd)
        outs.append(tc * cos + partner * sin_signed)
    return outs


def _expand_kv(chunks):
    lane = lax.broadcasted_iota(jnp.int32, chunks[0].shape, 1)
    low = lane < HEAD_DIM
    zero = jnp.zeros_like(chunks[0])
    outs = []
    for tc in chunks:
        sw = pltpu.roll(tc, HEAD_DIM, axis=1)
        outs += [jnp.where(low, tc, zero), jnp.where(low, zero, sw),
                 jnp.where(low, sw, zero), jnp.where(low, zero, tc)]
    return jnp.concatenate(outs, axis=1)


def _qkv_kernel(x_ref, g_ref, w_ref, gq_ref, gk_ref, cos_ref, sin_ref, seg_ref,
                q_ref, k_ref, vt_ref):
    scale = HEAD_DIM ** -0.5 * LOG2_E
    for r in range(x_ref.shape[0] // QKV_SUB_ROWS):
        rows = slice(r * QKV_SUB_ROWS, (r + 1) * QKV_SUB_ROWS)
        x = x_ref[rows, :]
        xn = (x * _rms_scale(x) * g_ref[...]).astype(BF16)
        qkv = jnp.dot(xn, w_ref[...], preferred_element_type=F32)
        q = qkv[:, :D_MODEL]
        k = qkv[:, D_MODEL:D_MODEL + KV_DIM]
        v = qkv[:, D_MODEL + KV_DIM:]
        cos = cos_ref[rows, :]
        sin = sin_ref[rows, :]
        qn = q * lax.rsqrt(_head_mean_square(q, seg_ref) + EPS) * gq_ref[...]
        kn = k * lax.rsqrt(_head_mean_square(k, seg_ref) + EPS) * gk_ref[...]
        q_ref[rows, :] = (jnp.concatenate(_rope(qn, cos, sin), axis=1) * scale).astype(BF16)
        k_ref[rows, :] = _expand_kv(_rope(kn, cos, sin)).astype(BF16)
        vt_ref[:, rows] = v.T.astype(BF16)


def _qkv_call(x, first, b, layer, g, w, gq, gk, cos, sin, seg):
    s = x.shape[1]
    tm = min(ROW_TILE, s)
    row = lambda width: pl.BlockSpec((None, tm, width), lambda bi, i: (bi, i, 0))
    x_spec = pl.BlockSpec((None, tm, D_MODEL), lambda bi, i: (bi + first, i, 0))
    tab = pl.BlockSpec((tm, LANES), lambda bi, i: (i, 0))
    out = jax.ShapeDtypeStruct((b, s, D_MODEL), BF16)
    return pl.pallas_call(
        _qkv_kernel,
        out_shape=(out, out, jax.ShapeDtypeStruct((b, KV_DIM, s), BF16)),
        grid=(b, s // tm),
        in_specs=[x_spec, _layer((1, D_MODEL), layer), _layer((D_MODEL, QKV_DIM), layer),
                  _layer((1, D_MODEL), layer), _layer((1, KV_DIM), layer), tab, tab,
                  _resident((MXU_DIM, MXU_DIM))],
        out_specs=(row(D_MODEL), row(D_MODEL),
                   pl.BlockSpec((None, KV_DIM, tm), lambda bi, i: (bi, 0, i))),
        compiler_params=_params(2),
        name="qkv_rope",
    )(x, g, w, gq, gk, cos, sin, seg)


def _attn_kernel(sink_ref, q_ref, kp_ref, kc_ref, kn_ref, vp_ref, vc_ref, vn_ref, bias_ref,
                 o_ref, *, layer):
    tq = q_ref.shape[0]
    i = pl.program_id(1)
    last = pl.num_programs(1) - 1
    n_sub = tq // BLOCK
    n_stage = n_sub * N_KV_HEADS
    keys = 3 * BLOCK
    row = lax.broadcasted_iota(jnp.int32, (LANES, 2 * BLOCK), 0)
    top = row < HEAD_DIM
    vzero = jnp.zeros((HEAD_DIM, keys), BF16)
    ones_row = lax.broadcasted_iota(jnp.int32, (2 * SUBLANES, 2 * keys), 0)
    ones_col = lax.broadcasted_iota(jnp.int32, (2 * SUBLANES, 2 * keys), 1)
    ones_rows = jnp.where((ones_row == 0) & (ones_col < keys) | (ones_row == 1) & (ones_col >= keys),
                          1.0, 0.0).astype(BF16)

    def key_rows(j, lanes):
        parts = []
        if j == 0:
            parts.append(kp_ref[:, lanes])
        parts.append(kc_ref[max(j - 1, 0) * BLOCK:min(j + 2, n_sub) * BLOCK, lanes])
        if j == n_sub - 1:
            parts.append(kn_ref[:, lanes])
        return jnp.concatenate(parts, axis=0)

    def value_cols(j, rows):
        parts = []
        if j == 0:
            parts.append(vp_ref[rows, :])
        parts.append(vc_ref[rows, max(j - 1, 0) * BLOCK:min(j + 2, n_sub) * BLOCK])
        if j == n_sub - 1:
            parts.append(vn_ref[rows, :])
        return jnp.concatenate(parts, axis=1)

    def bias_for(j):
        b0 = bias_ref[0]
        if j == 0:
            b0 = jnp.where(i == 0, bias_ref[1], b0)
        if j == n_sub - 1:
            b0 = jnp.where(i == last, bias_ref[2], b0)
        return b0

    def scores(t):
        j, h = divmod(t, N_KV_HEADS)
        c0 = h * 2 * LANES
        qrows = slice(j * BLOCK, (j + 1) * BLOCK)
        q2 = jnp.concatenate([q_ref[qrows, c0:c0 + LANES],
                              q_ref[qrows, c0 + LANES:c0 + 2 * LANES]], axis=0)
        kab = jnp.concatenate([key_rows(j, slice(c0, c0 + LANES)),
                               key_rows(j, slice(c0 + LANES, c0 + 2 * LANES))], axis=0)
        return lax.dot_general(kab, q2, (((1,), (1,)), ((), ())), preferred_element_type=F32)

    st_next = scores(0)
    bias = None
    for t in range(n_stage):
        j, h = divmod(t, N_KV_HEADS)
        if h == 0:
            bias = bias_for(j)
        c0 = h * 2 * LANES
        qrows = slice(j * BLOCK, (j + 1) * BLOCK)
        st = st_next
        if t + 1 < n_stage:
            st_next = scores(t + 1)
        vt = value_cols(j, slice(h * HEAD_DIM, (h + 1) * HEAD_DIM))
        vabt = jnp.concatenate([jnp.concatenate([vt, vzero], axis=1),
                                jnp.concatenate([vzero, vt], axis=1),
                                ones_rows], axis=0)
        p_rows = []
        e_rows = []
        for ab in range(2):
            p_cols = []
            e_cols = []
            for ch in range(2):
                sink = sink_ref[layer, 4 * h + 2 * ch + ab] * LOG2_E
                sc = st[ab * keys:(ab + 1) * keys, ch * BLOCK:(ch + 1) * BLOCK]
                sc = jnp.concatenate([sc[:BLOCK] + bias[:BLOCK], sc[BLOCK:2 * BLOCK],
                                      sc[2 * BLOCK:] + bias[2 * BLOCK:]], axis=0)
                m = jnp.maximum(jnp.max(sc, axis=0, keepdims=True), sink)
                e_cols.append(jnp.exp2(sink - m))
                p_cols.append(jnp.exp2(sc - m).astype(BF16))
            p_rows.append(jnp.concatenate(p_cols, axis=1))
            e_rows.append(jnp.concatenate(e_cols, axis=1))
        pt = jnp.concatenate(p_rows, axis=0)
        o2t = jnp.dot(vabt, pt, preferred_element_type=F32)
        inv_a = 1.0 / (o2t[LANES:LANES + 1] + e_rows[0])
        inv_b = 1.0 / (o2t[LANES + 1:LANES + 2] + e_rows[1])
        o2 = (o2t[:LANES] * jnp.where(top, inv_a, inv_b)).T
        o_ref[qrows, c0:c0 + LANES] = o2[:BLOCK].astype(BF16)
        o_ref[qrows, c0 + LANES:c0 + 2 * LANES] = o2[BLOCK:].astype(BF16)


def _attn_call(sinks, layer, q, k, vt, bias_t):
    b, s, _ = q.shape
    tq = min(ATTN_Q_TILE, s)
    r = tq // BLOCK
    nb = s // BLOCK
    assert s % tq == 0 and s >= 2 * BLOCK
    prev_i = lambda i: jnp.maximum(i * r - 1, 0)
    next_i = lambda i: jnp.minimum((i + 1) * r, nb - 1)
    cur = pl.BlockSpec((None, tq, D_MODEL), lambda bi, i: (bi, i, 0))
    prev = pl.BlockSpec((None, BLOCK, D_MODEL), lambda bi, i: (bi, prev_i(i), 0))
    nxt = pl.BlockSpec((None, BLOCK, D_MODEL), lambda bi, i: (bi, next_i(i), 0))
    vcur = pl.BlockSpec((None, KV_DIM, tq), lambda bi, i: (bi, 0, i))
    vprev = pl.BlockSpec((None, KV_DIM, BLOCK), lambda bi, i: (bi, 0, prev_i(i)))
    vnxt = pl.BlockSpec((None, KV_DIM, BLOCK), lambda bi, i: (bi, 0, next_i(i)))
    return pl.pallas_call(
        functools.partial(_attn_kernel, layer=layer),
        out_shape=jax.ShapeDtypeStruct((b, s, D_MODEL), BF16),
        grid=(b, s // tq),
        in_specs=[pl.BlockSpec(memory_space=pltpu.SMEM), cur, prev, cur, nxt, vprev, vcur, vnxt,
                  _resident((3, 3 * BLOCK, BLOCK))],
        out_specs=cur,
        compiler_params=_params(2),
        name="band_attention",
    )(sinks, q, k, k, k, vt, vt, vt, bias_t)


def _proj_ffn_rows(x, mix, wp_ref, g_ref, wgu_ref, wd_ref):
    x1 = x + jnp.dot(mix.astype(BF16), wp_ref[...], preferred_element_type=F32)
    xn = (x1 * _rms_scale(x1) * g_ref[...]).astype(BF16)
    acc = x1
    for c in range(D_FF // FFN_CHUNK):
        lo = c * FFN_CHUNK
        gate = jnp.dot(xn, wgu_ref[:, lo:lo + FFN_CHUNK], preferred_element_type=F32)
        up = jnp.dot(xn, wgu_ref[:, D_FF + lo:D_FF + lo + FFN_CHUNK], preferred_element_type=F32)
        hid = (gate * jax.nn.sigmoid(gate) * up).astype(BF16)
        acc = acc + jnp.dot(hid, wd_ref[lo:lo + FFN_CHUNK, :], preferred_element_type=F32)
    return acc


def _ffn_weight_specs(mixer_layer, layer):
    return [_layer((D_MODEL, D_MODEL), mixer_layer), _layer((1, D_MODEL), layer),
            _layer((D_MODEL, 2 * D_FF), layer), _layer((D_FF, D_MODEL), layer)]


def _pick(i, n0, refs):
    if len(refs) == 1:
        return refs[0][...]
    return jnp.where(i < n0, refs[0][...], refs[1][...])


def _proj_ffn_kernel(*refs, n_x, n_mix, n0):
    x_refs, mix_refs = refs[:n_x], refs[n_x:n_x + n_mix]
    wp_ref, g_ref, wgu_ref, wd_ref, o_ref = refs[n_x + n_mix:]
    i = pl.program_id(0)
    o_ref[...] = _proj_ffn_rows(_pick(i, n0, x_refs), _pick(i, n0, mix_refs), wp_ref, g_ref, wgu_ref, wd_ref)


def _row_specs(arrays, tm, first_tile):
    if len(arrays) == 1:
        return [pl.BlockSpec((tm, D_MODEL), lambda i: (i + first_tile, 0))]
    n0 = arrays[0].shape[0] // tm
    return [pl.BlockSpec((tm, D_MODEL), lambda i: (jnp.minimum(i, n0 - 1), 0)),
            pl.BlockSpec((tm, D_MODEL), lambda i: (jnp.maximum(i - n0, 0), 0))]


def _proj_ffn_call(xs, mixes, mixer_layer, layer, wp, g, wgu, wd, x_first_row=0):
    t = sum(m.shape[0] for m in mixes)
    tm = min(ROW_TILE, mixes[0].shape[0])
    n0 = mixes[0].shape[0] // tm
    assert all(a.shape[0] % tm == 0 for a in xs + mixes) and x_first_row % tm == 0
    assert len(xs) == 1 or [a.shape[0] for a in xs] == [m.shape[0] for m in mixes]
    return pl.pallas_call(
        functools.partial(_proj_ffn_kernel, n_x=len(xs), n_mix=len(mixes), n0=n0),
        out_shape=jax.ShapeDtypeStruct((t, D_MODEL), F32),
        grid=(t // tm,),
        in_specs=_row_specs(xs, tm, x_first_row // tm) + _row_specs(mixes, tm, 0)
        + _ffn_weight_specs(mixer_layer, layer),
        out_specs=pl.BlockSpec((tm, D_MODEL), lambda i: (i, 0)),
        compiler_params=_params(1),
        name="proj_ffn",
    )(*xs, *mixes, wp, g, wgu, wd)


def _dft1_kernel(x_ref, g_ref, wc_ref, m1_ref, y_ref):
    n1 = x_ref.shape[0]
    gd = FOURIER_GROUP_DIM
    for h in range(SUBLANES // DFT_COLS):
        js = range(h * DFT_COLS, (h + 1) * DFT_COLS)
        xs = jnp.concatenate([x_ref[:, j, :] for j in js], axis=0)
        xn = (xs * _rms_scale(xs) * g_ref[...]).astype(BF16)
        z = [jnp.dot(xn[:, gi * gd:(gi + 1) * gd], wc_ref[...], preferred_element_type=F32)
             for gi in range(N_FOURIER_GROUPS)]
        zr = jnp.concatenate([zg[:, :gd] for zg in z], axis=1).astype(BF16)
        zi = jnp.concatenate([zg[:, gd:] for zg in z], axis=1).astype(BF16)
        rhs = jnp.concatenate(
            [jnp.concatenate([zr[jj * n1:(jj + 1) * n1], zi[jj * n1:(jj + 1) * n1]], axis=0)
             for jj in range(DFT_COLS)], axis=1)
        y = jnp.dot(m1_ref[...], rhs, preferred_element_type=F32).astype(BF16)
        yw = pltpu.bitcast(y, jnp.uint32)
        for jj, j in enumerate(js):
            y_ref[j] = yw[:, jj * D_MODEL:(jj + 1) * D_MODEL]


def _dft1_call(x, first, b, layer, g, wc, m1):
    n, s, _ = x.shape
    n2 = BLOCK
    n1 = s // n2
    return pl.pallas_call(
        _dft1_kernel,
        out_shape=jax.ShapeDtypeStruct((b, n2, n1, D_MODEL), jnp.uint32),
        grid=(b, n2 // SUBLANES),
        in_specs=[pl.BlockSpec((None, n1, SUBLANES, D_MODEL), lambda bi, i: (bi + first, 0, i, 0)),
                  _layer((1, D_MODEL), layer),
                  _resident((FOURIER_GROUP_DIM, 2 * FOURIER_GROUP_DIM)),
                  _resident((2 * n1, 2 * n1))],
        out_specs=pl.BlockSpec((None, SUBLANES, n1, D_MODEL), lambda bi, i: (bi, i, 0, 0)),
        compiler_params=_params(2),
        name="dft_stage1",
    )(x.reshape(n, n1, n2, D_MODEL), g, wc, m1)


def _dft2_kernel(y_ref, m2_ref, o_ref):
    for j in range(SUBLANES):
        rhs = pltpu.bitcast(y_ref[:, j, :], BF16)
        o_ref[:, j, :] = jnp.dot(m2_ref[j], rhs, preferred_element_type=F32)


def _dft2_call(y, m2, s):
    b = y.shape[0]
    n2 = BLOCK
    n1 = s // n2
    blk = pl.BlockSpec((None, n2, SUBLANES, D_MODEL), lambda bi, i: (bi, 0, i, 0))
    return pl.pallas_call(
        _dft2_kernel,
        out_shape=jax.ShapeDtypeStruct((b, n2, n1, D_MODEL), F32),
        grid=(b, n1 // SUBLANES),
        in_specs=[blk, pl.BlockSpec((SUBLANES, n2, 2 * n2), lambda bi, i: (i, 0, 0))],
        out_specs=blk,
        compiler_params=_params(2),
        name="dft_stage2",
    )(y, m2).reshape(b, s, D_MODEL)


def _rope_tables(s):
    half = HEAD_DIM // 2
    inv_freq = ROPE_THETA ** (-jnp.arange(half, dtype=F32) / half)
    ang = jnp.arange(s).astype(F32)[:, None] * inv_freq[None, :]
    cos = jnp.cos(ang)
    sin = jnp.sin(ang)
    reps = LANES // HEAD_DIM
    cos_t = jnp.tile(jnp.concatenate([cos, cos], axis=1), (1, reps))
    sin_t = jnp.tile(jnp.concatenate([-sin, sin], axis=1), (1, reps))
    return cos_t, sin_t


def _band_bias():
    qi = np.arange(BLOCK)[:, None]
    kj = np.arange(3 * BLOCK)[None, :]
    band = np.abs(kj - BLOCK - qi) <= WINDOW
    mid = np.where(band, 0.0, NEG_INF)
    first = np.where(band & (kj >= BLOCK), 0.0, NEG_INF)
    last = np.where(band & (kj < 2 * BLOCK), 0.0, NEG_INF)
    return jnp.asarray(np.stack([mid.T, first.T, last.T]), dtype=F32)


def _segment_ones():
    idx = np.arange(MXU_DIM) // HEAD_DIM
    return jnp.asarray(idx[:, None] == idx[None, :], dtype=BF16)


def _channel_dft():
    c = np.arange(FOURIER_GROUP_DIM)
    ang = 2.0 * np.pi * ((c[:, None] * c[None, :]) % FOURIER_GROUP_DIM) / FOURIER_GROUP_DIM
    return jnp.asarray(np.concatenate([np.cos(ang), -np.sin(ang)], axis=1), dtype=F32).astype(BF16)


def _stage1_dft(n1):
    k = np.arange(n1)
    ang = 2.0 * np.pi * ((k[:, None] * k[None, :]) % n1) / n1
    c, s = np.cos(ang), np.sin(ang)
    m = np.stack([np.concatenate([c, s], axis=1), np.concatenate([-s, c], axis=1)], axis=1)
    return jnp.asarray(m.reshape(2 * n1, 2 * n1), dtype=F32).astype(BF16)


def _stage2_dft(s):
    n2 = BLOCK
    n1 = s // n2
    k1 = np.arange(n1)[:, None]
    k2 = np.arange(n2)[:, None]
    n = np.arange(n2)[None, :]
    a1 = 2.0 * np.pi * ((k1 * n) % s) / s
    a2 = 2.0 * np.pi * ((k2 * n) % n2) / n2
    c1, s1 = jnp.asarray(np.cos(a1), F32)[:, None, :], jnp.asarray(np.sin(a1), F32)[:, None, :]
    c2, s2 = jnp.asarray(np.cos(a2), F32)[None], jnp.asarray(np.sin(a2), F32)[None]
    scale = float(1.0 / np.sqrt(float(s) * FOURIER_GROUP_DIM))
    cos = (c1 * c2 - s1 * s2) * scale
    sin = (s1 * c2 + c1 * s2) * scale
    return jnp.stack([cos, sin], axis=3).reshape(n1, n2, 2 * n2).astype(BF16)


def _sequences(xs, shapes, t):
    b, s = shapes[t]
    if len(xs) > 1:
        return xs[t].reshape(b, s, D_MODEL), 0
    rows_before = sum(bb * ss for bb, ss in shapes[:t])
    return xs[0].reshape(-1, s, D_MODEL), rows_before // s


def kernel(x_prompt, x_sample, attn_norm_g, w_qkv, q_norm_g, k_norm_g, attn_sinks, w_o_attn,
           fourier_norm_g, w_fourier_out, ffn_norm_g, w_gate_up, w_down):
    p = {
        "attn_norm_g": attn_norm_g[:, None, :],
        "w_qkv": w_qkv.astype(BF16),
        "q_norm_g": jnp.tile(q_norm_g, (1, N_Q_HEADS))[:, None, :],
        "k_norm_g": jnp.tile(k_norm_g, (1, N_KV_HEADS))[:, None, :],
        "attn_sinks": attn_sinks,
        "w_o_attn": w_o_attn.astype(BF16),
        "fourier_norm_g": fourier_norm_g[:, None, :],
        "w_fourier_out": w_fourier_out.astype(BF16),
        "ffn_norm_g": ffn_norm_g[:, None, :],
        "w_gate_up": w_gate_up.astype(BF16),
        "w_down": w_down.astype(BF16),
    }
    bias, seg, wc = _band_bias(), _segment_ones(), _channel_dft()
    shapes = [x.shape[:2] for x in (x_prompt, x_sample)]
    tables = [(_rope_tables(s), _stage1_dft(s // BLOCK), _stage2_dft(s)) for _, s in shapes]
    xs = [x.reshape(-1, D_MODEL) for x in (x_prompt, x_sample)]
    for i in range(DEPTH):
        j = i // 2
        mixes = []
        for t, (b, s) in enumerate(shapes):
            (cos, sin), m1, m2 = tables[t]
            x, first = _sequences(xs, shapes, t)
            if i % 2 == 0:
                q, k, vt = _qkv_call(x, first, b, j, p["attn_norm_g"], p["w_qkv"], p["q_norm_g"],
                                     p["k_norm_g"], cos, sin, seg)
                mix = _attn_call(p["attn_sinks"], j, q, k, vt, bias)
            else:
                mix = _dft2_call(_dft1_call(x, first, b, j, p["fourier_norm_g"], wc, m1), m2, s)
            mixes.append(mix.reshape(b * s, D_MODEL))
        wp = p["w_o_attn"] if i % 2 == 0 else p["w_fourier_out"]
        ffn = functools.partial(_proj_ffn_call, mixer_layer=j, layer=i, wp=wp, g=p["ffn_norm_g"],
                                wgu=p["w_gate_up"], wd=p["w_down"])
        if i + 1 < DEPTH:
            xs = [ffn(xs, mixes)]
        else:
            rows = [b * s for b, s in shapes]
            outs = [ffn(xs, [mixes[t]], x_first_row=sum(rows[:t])) for t in range(len(shapes))]
    return tuple(o.reshape(b, s, D_MODEL) for o, (b, s) in zip(outs, shapes))
```

```python
import functools

import numpy as np
import jax
import jax.numpy as jnp
from jax import lax
from jax.experimental import pallas as pl
from jax.experimental.pallas import tpu as pltpu

D_MODEL = 1024
HEAD_DIM = 64
N_Q_HEADS = 16
N_KV_HEADS = 4
QKV_DIM = (N_Q_HEADS + 2 * N_KV_HEADS) * HEAD_DIM
KV_DIM = N_KV_HEADS * HEAD_DIM
WINDOW = 128
BLOCK = 128
ROPE_THETA = 10000.0
N_FOURIER_GROUPS = 4
FOURIER_GROUP_DIM = D_MODEL // N_FOURIER_GROUPS
D_FF = 2816
EPS = 1e-6
NEG_INF = -1e30
DEPTH = 4

LANES = 128
SUBLANES = 8
MXU_DIM = 256
VMEM_LIMIT_BYTES = 56 * 1024 * 1024

ROW_TILE = 1024
FFN_ROW_TILE = 1024
ATTN_Q_TILE = 512
QKV_SUB_ROWS = 128
FFN_CHUNK = 256
DFT_COLS = 4

LOG2_E = 1.4426950408889634
BF16 = jnp.bfloat16
F32 = jnp.float32


def _resident(shape):
    nd = len(shape)
    return pl.BlockSpec(shape, lambda *_: (0,) * nd, pipeline_mode=pl.Buffered(1))


def _layer(shape, layer):
    nd = len(shape)
    return pl.BlockSpec((None,) + tuple(shape), lambda *_: (layer,) + (0,) * nd,
                        pipeline_mode=pl.Buffered(1))


def _params(n_axes, flags=None):
    return pltpu.CompilerParams(
        dimension_semantics=("arbitrary",) * n_axes,
        vmem_limit_bytes=VMEM_LIMIT_BYTES,
        flags=flags,
    )


def _rms_scale(x):
    return lax.rsqrt(jnp.mean(x * x, axis=-1, keepdims=True) + EPS)


def _head_mean_square(t, seg_ref):
    sq = (t * t).astype(BF16)
    seg = seg_ref[...]
    cols = []
    for c in range(t.shape[1] // MXU_DIM):
        sl = slice(c * MXU_DIM, (c + 1) * MXU_DIM)
        cols.append(jnp.dot(sq[:, sl], seg, preferred_element_type=F32))
    ss = cols[0] if len(cols) == 1 else jnp.concatenate(cols, axis=1)
    return ss * (1.0 / HEAD_DIM)


def _rope(t, cos, sin_signed):
    half = HEAD_DIM // 2
    lane = lax.broadcasted_iota(jnp.int32, (t.shape[0], LANES), 1)
    first_half = (lane % HEAD_DIM) < half
    outs = []
    for c in range(t.shape[1] // LANES):
        tc = t[:, c * LANES:(c + 1) * LANES]
        fwd = pltpu.roll(tc, LANES - half, axis=1)
        bwd = pltpu.roll(tc, half, axis=1)
        partner = jnp.where(first_half, fwd, bwd)
        outs.append(tc * cos + partner * sin_signed)
    return outs


def _expand_kv(chunks):
    lane = lax.broadcasted_iota(jnp.int32, chunks[0].shape, 1)
    low = lane < HEAD_DIM
    zero = jnp.zeros_like(chunks[0])
    outs = []
    for tc in chunks:
        sw = pltpu.roll(tc, HEAD_DIM, axis=1)
        outs += [jnp.where(low, tc, zero), jnp.where(low, zero, sw),
                 jnp.where(low, sw, zero), jnp.where(low, zero, tc)]
    return jnp.concatenate(outs, axis=1)


def _qkv_kernel(x_ref, g_ref, w_ref, gq_ref, gk_ref, cos_ref, sin_ref, seg_ref,
                q_ref, k_ref, vt_ref):
    scale = HEAD_DIM ** -0.5 * LOG2_E
    for r in range(x_ref.shape[0] // QKV_SUB_ROWS):
        rows = slice(r * QKV_SUB_ROWS, (r + 1) * QKV_SUB_ROWS)
        x = x_ref[rows, :]
        xn = (x * _rms_scale(x) * g_ref[...]).astype(BF16)
        qkv = jnp.dot(xn, w_ref[...], preferred_element_type=F32)
        q = qkv[:, :D_MODEL]
        k = qkv[:, D_MODEL:D_MODEL + KV_DIM]
        v = qkv[:, D_MODEL + KV_DIM:]
        cos = cos_ref[rows, :]
        sin = sin_ref[rows, :]
        qn = q * lax.rsqrt(_head_mean_square(q, seg_ref) + EPS) * gq_ref[...]
        kn = k * lax.rsqrt(_head_mean_square(k, seg_ref) + EPS) * gk_ref[...]
        q_ref[rows, :] = (jnp.concatenate(_rope(qn, cos, sin), axis=1) * scale).astype(BF16)
        k_ref[rows, :] = _expand_kv(_rope(kn, cos, sin)).astype(BF16)
        vt_ref[:, rows] = v.T.astype(BF16)


def _qkv_call(x, layer, g, w, gq, gk, cos, sin, seg):
    b, s, _ = x.shape
    tm = min(ROW_TILE, s)
    row = lambda width: pl.BlockSpec((None, tm, width), lambda bi, i: (bi, i, 0))
    tab = pl.BlockSpec((tm, LANES), lambda bi, i: (i, 0))
    out = jax.ShapeDtypeStruct((b, s, D_MODEL), BF16)
    return pl.pallas_call(
        _qkv_kernel,
        out_shape=(out, out, jax.ShapeDtypeStruct((b, KV_DIM, s), BF16)),
        grid=(b, s // tm),
        in_specs=[row(D_MODEL), _layer((1, D_MODEL), layer), _layer((D_MODEL, QKV_DIM), layer),
                  _layer((1, D_MODEL), layer), _layer((1, KV_DIM), layer), tab, tab,
                  _resident((MXU_DIM, MXU_DIM))],
        out_specs=(row(D_MODEL), row(D_MODEL),
                   pl.BlockSpec((None, KV_DIM, tm), lambda bi, i: (bi, 0, i))),
        compiler_params=_params(2),
        name="qkv_rope",
    )(x, g, w, gq, gk, cos, sin, seg)


def _attn_kernel(sink_ref, q_ref, kp_ref, kc_ref, kn_ref, vp_ref, vc_ref, vn_ref, bias_ref,
                 o_ref, *, layer):
    tq = q_ref.shape[0]
    i = pl.program_id(1)
    last = pl.num_programs(1) - 1
    n_sub = tq // BLOCK
    n_stage = n_sub * N_KV_HEADS
    keys = 3 * BLOCK
    row = lax.broadcasted_iota(jnp.int32, (LANES, 2 * BLOCK), 0)
    top = row < HEAD_DIM
    vzero = jnp.zeros((HEAD_DIM, keys), BF16)
    ones_row = lax.broadcasted_iota(jnp.int32, (2 * SUBLANES, 2 * keys), 0)
    ones_col = lax.broadcasted_iota(jnp.int32, (2 * SUBLANES, 2 * keys), 1)
    ones_rows = jnp.where((ones_row == 0) & (ones_col < keys) | (ones_row == 1) & (ones_col >= keys),
                          1.0, 0.0).astype(BF16)

    def key_rows(j, lanes):
        parts = []
        if j == 0:
            parts.append(kp_ref[:, lanes])
        parts.append(kc_ref[max(j - 1, 0) * BLOCK:min(j + 2, n_sub) * BLOCK, lanes])
        if j == n_sub - 1:
            parts.append(kn_ref[:, lanes])
        return jnp.concatenate(parts, axis=0)

    def value_cols(j, rows):
        parts = []
        if j == 0:
            parts.append(vp_ref[rows, :])
        parts.append(vc_ref[rows, max(j - 1, 0) * BLOCK:min(j + 2, n_sub) * BLOCK])
        if j == n_sub - 1:
            parts.append(vn_ref[rows, :])
        return jnp.concatenate(parts, axis=1)

    def bias_for(j):
        b0 = bias_ref[0]
        if j == 0:
            b0 = jnp.where(i == 0, bias_ref[1], b0)
        if j == n_sub - 1:
            b0 = jnp.where(i == last, bias_ref[2], b0)
        return b0

    def scores(t):
        j, h = divmod(t, N_KV_HEADS)
        c0 = h * 2 * LANES
        qrows = slice(j * BLOCK, (j + 1) * BLOCK)
        q2 = jnp.concatenate([q_ref[qrows, c0:c0 + LANES],
                              q_ref[qrows, c0 + LANES:c0 + 2 * LANES]], axis=0)
        kab = jnp.concatenate([key_rows(j, slice(c0, c0 + LANES)),
                               key_rows(j, slice(c0 + LANES, c0 + 2 * LANES))], axis=0)
        return lax.dot_general(kab, q2, (((1,), (1,)), ((), ())), preferred_element_type=F32)

    st_next = scores(0)
    bias = None
    for t in range(n_stage):
        j, h = divmod(t, N_KV_HEADS)
        if h == 0:
            bias = bias_for(j)
        c0 = h * 2 * LANES
        qrows = slice(j * BLOCK, (j + 1) * BLOCK)
        st = st_next
        if t + 1 < n_stage:
            st_next = scores(t + 1)
        vt = value_cols(j, slice(h * HEAD_DIM, (h + 1) * HEAD_DIM))
        vabt = jnp.concatenate([jnp.concatenate([vt, vzero], axis=1),
                                jnp.concatenate([vzero, vt], axis=1),
                                ones_rows], axis=0)
        p_rows = []
        e_rows = []
        for ab in range(2):
            p_cols = []
            e_cols = []
            for ch in range(2):
                sink = sink_ref[layer, 4 * h + 2 * ch + ab] * LOG2_E
                sc = st[ab * keys:(ab + 1) * keys, ch * BLOCK:(ch + 1) * BLOCK]
                sc = jnp.concatenate([sc[:BLOCK] + bias[:BLOCK], sc[BLOCK:2 * BLOCK],
                                      sc[2 * BLOCK:] + bias[2 * BLOCK:]], axis=0)
                m = jnp.maximum(jnp.max(sc, axis=0, keepdims=True), sink)
                e_cols.append(jnp.exp2(sink - m))
                p_cols.append(jnp.exp2(sc - m).astype(BF16))
            p_rows.append(jnp.concatenate(p_cols, axis=1))
            e_rows.append(jnp.concatenate(e_cols, axis=1))
        pt = jnp.concatenate(p_rows, axis=0)
        o2t = jnp.dot(vabt, pt, preferred_element_type=F32)
        inv_a = 1.0 / (o2t[LANES:LANES + 1] + e_rows[0])
        inv_b = 1.0 / (o2t[LANES + 1:LANES + 2] + e_rows[1])
        o2 = (o2t[:LANES] * jnp.where(top, inv_a, inv_b)).T
        o_ref[qrows, c0:c0 + LANES] = o2[:BLOCK].astype(BF16)
        o_ref[qrows, c0 + LANES:c0 + 2 * LANES] = o2[BLOCK:].astype(BF16)


def _attn_call(sinks, layer, q, k, vt, bias_t):
    b, s, _ = q.shape
    tq = min(ATTN_Q_TILE, s)
    r = tq // BLOCK
    nb = s // BLOCK
    assert s % tq == 0 and s >= 2 * BLOCK
    prev_i = lambda i: jnp.maximum(i * r - 1, 0)
    next_i = lambda i: jnp.minimum((i + 1) * r, nb - 1)
    cur = pl.BlockSpec((None, tq, D_MODEL), lambda bi, i: (bi, i, 0))
    prev = pl.BlockSpec((None, BLOCK, D_MODEL), lambda bi, i: (bi, prev_i(i), 0))
    nxt = pl.BlockSpec((None, BLOCK, D_MODEL), lambda bi, i: (bi, next_i(i), 0))
    vcur = pl.BlockSpec((None, KV_DIM, tq), lambda bi, i: (bi, 0, i))
    vprev = pl.BlockSpec((None, KV_DIM, BLOCK), lambda bi, i: (bi, 0, prev_i(i)))
    vnxt = pl.BlockSpec((None, KV_DIM, BLOCK), lambda bi, i: (bi, 0, next_i(i)))
    return pl.pallas_call(
        functools.partial(_attn_kernel, layer=layer),
        out_shape=jax.ShapeDtypeStruct((b, s, D_MODEL), BF16),
        grid=(b, s // tq),
        in_specs=[pl.BlockSpec(memory_space=pltpu.SMEM), cur, prev, cur, nxt, vprev, vcur, vnxt,
                  _resident((3, 3 * BLOCK, BLOCK))],
        out_specs=cur,
        compiler_params=_params(2),
        name="band_attention",
    )(sinks, q, k, k, k, vt, vt, vt, bias_t)


def _proj_ffn_rows(x, mix, wp_ref, g_ref, wgu_ref, wd_ref):
    x1 = x + jnp.dot(mix.astype(BF16), wp_ref[...], preferred_element_type=F32)
    xn = (x1 * _rms_scale(x1) * g_ref[...]).astype(BF16)
    acc = x1
    for c in range(D_FF // FFN_CHUNK):
        lo = c * FFN_CHUNK
        gate = jnp.dot(xn, wgu_ref[:, lo:lo + FFN_CHUNK], preferred_element_type=F32)
        up = jnp.dot(xn, wgu_ref[:, D_FF + lo:D_FF + lo + FFN_CHUNK], preferred_element_type=F32)
        hid = (gate * jax.nn.sigmoid(gate) * up).astype(BF16)
        acc = acc + jnp.dot(hid, wd_ref[lo:lo + FFN_CHUNK, :], preferred_element_type=F32)
    return acc


def _ffn_weight_specs(mixer_layer, layer):
    return [_layer((D_MODEL, D_MODEL), mixer_layer), _layer((1, D_MODEL), layer),
            _layer((D_MODEL, 2 * D_FF), layer), _layer((D_FF, D_MODEL), layer)]


def _proj_ffn_kernel(x_ref, mix_ref, wp_ref, g_ref, wgu_ref, wd_ref, o_ref):
    o_ref[...] = _proj_ffn_rows(x_ref[...], mix_ref[...], wp_ref, g_ref, wgu_ref, wd_ref)


def _proj_ffn_call(x, mix, mixer_layer, layer, wp, g, wgu, wd):
    b, s, _ = x.shape
    t = b * s
    tm = min(FFN_ROW_TILE, t)
    row = pl.BlockSpec((tm, D_MODEL), lambda i: (i, 0))
    return pl.pallas_call(
        _proj_ffn_kernel,
        out_shape=jax.ShapeDtypeStruct((t, D_MODEL), F32),
        grid=(t // tm,),
        in_specs=[row, row] + _ffn_weight_specs(mixer_layer, layer),
        out_specs=row,
        compiler_params=_params(1),
        name="proj_ffn",
    )(x.reshape(t, D_MODEL), mix.reshape(t, D_MODEL), wp, g, wgu, wd).reshape(b, s, D_MODEL)


def _dft1_kernel(x_ref, g_ref, wc_ref, m1_ref, y_ref):
    n1 = x_ref.shape[0]
    gd = FOURIER_GROUP_DIM
    for h in range(SUBLANES // DFT_COLS):
        js = range(h * DFT_COLS, (h + 1) * DFT_COLS)
        xs = jnp.concatenate([x_ref[:, j, :] for j in js], axis=0)
        xn = (xs * _rms_scale(xs) * g_ref[...]).astype(BF16)
        z = [jnp.dot(xn[:, gi * gd:(gi + 1) * gd], wc_ref[...], preferred_element_type=F32)
             for gi in range(N_FOURIER_GROUPS)]
        zr = jnp.concatenate([zg[:, :gd] for zg in z], axis=1).astype(BF16)
        zi = jnp.concatenate([zg[:, gd:] for zg in z], axis=1).astype(BF16)
        rhs = jnp.concatenate(
            [jnp.concatenate([zr[jj * n1:(jj + 1) * n1], zi[jj * n1:(jj + 1) * n1]], axis=0)
             for jj in range(DFT_COLS)], axis=1)
        y = jnp.dot(m1_ref[...], rhs, preferred_element_type=F32).astype(BF16)
        yw = pltpu.bitcast(y, jnp.uint32)
        for jj, j in enumerate(js):
            y_ref[j] = yw[:, jj * D_MODEL:(jj + 1) * D_MODEL]


def _dft1_call(x, layer, g, wc, m1):
    b, s, _ = x.shape
    n2 = BLOCK
    n1 = s // n2
    return pl.pallas_call(
        _dft1_kernel,
        out_shape=jax.ShapeDtypeStruct((b, n2, n1, D_MODEL), jnp.uint32),
        grid=(b, n2 // SUBLANES),
        in_specs=[pl.BlockSpec((None, n1, SUBLANES, D_MODEL), lambda bi, i: (bi, 0, i, 0)),
                  _layer((1, D_MODEL), layer),
                  _resident((FOURIER_GROUP_DIM, 2 * FOURIER_GROUP_DIM)),
                  _resident((2 * n1, 2 * n1))],
        out_specs=pl.BlockSpec((None, SUBLANES, n1, D_MODEL), lambda bi, i: (bi, i, 0, 0)),
        compiler_params=_params(2),
        name="dft_stage1",
    )(x.reshape(b, n1, n2, D_MODEL), g, wc, m1)


def _dft2_kernel(y_ref, m2_ref, o_ref):
    for j in range(SUBLANES):
        rhs = pltpu.bitcast(y_ref[:, j, :], BF16)
        o_ref[:, j, :] = jnp.dot(m2_ref[j], rhs, preferred_element_type=F32)


def _dft2_call(y, m2, s):
    b = y.shape[0]
    n2 = BLOCK
    n1 = s // n2
    blk = pl.BlockSpec((None, n2, SUBLANES, D_MODEL), lambda bi, i: (bi, 0, i, 0))
    return pl.pallas_call(
        _dft2_kernel,
        out_shape=jax.ShapeDtypeStruct((b, n2, n1, D_MODEL), F32),
        grid=(b, n1 // SUBLANES),
        in_specs=[blk, pl.BlockSpec((SUBLANES, n2, 2 * n2), lambda bi, i: (i, 0, 0))],
        out_specs=blk,
        compiler_params=_params(2),
        name="dft_stage2",
    )(y, m2).reshape(b, s, D_MODEL)


def _rope_tables(s):
    half = HEAD_DIM // 2
    inv_freq = ROPE_THETA ** (-jnp.arange(half, dtype=F32) / half)
    ang = jnp.arange(s).astype(F32)[:, None] * inv_freq[None, :]
    cos = jnp.cos(ang)
    sin = jnp.sin(ang)
    reps = LANES // HEAD_DIM
    cos_t = jnp.tile(jnp.concatenate([cos, cos], axis=1), (1, reps))
    sin_t = jnp.tile(jnp.concatenate([-sin, sin], axis=1), (1, reps))
    return cos_t, sin_t


def _band_bias():
    qi = np.arange(BLOCK)[:, None]
    kj = np.arange(3 * BLOCK)[None, :]
    band = np.abs(kj - BLOCK - qi) <= WINDOW
    mid = np.where(band, 0.0, NEG_INF)
    first = np.where(band & (kj >= BLOCK), 0.0, NEG_INF)
    last = np.where(band & (kj < 2 * BLOCK), 0.0, NEG_INF)
    return jnp.asarray(np.stack([mid.T, first.T, last.T]), dtype=F32)


def _segment_ones():
    idx = np.arange(MXU_DIM) // HEAD_DIM
    return jnp.asarray(idx[:, None] == idx[None, :], dtype=BF16)


def _channel_dft():
    c = np.arange(FOURIER_GROUP_DIM)
    ang = 2.0 * np.pi * ((c[:, None] * c[None, :]) % FOURIER_GROUP_DIM) / FOURIER_GROUP_DIM
    return jnp.asarray(np.concatenate([np.cos(ang), -np.sin(ang)], axis=1), dtype=F32).astype(BF16)


def _stage1_dft(n1):
    k = np.arange(n1)
    ang = 2.0 * np.pi * ((k[:, None] * k[None, :]) % n1) / n1
    c, s = np.cos(ang), np.sin(ang)
    m = np.stack([np.concatenate([c, s], axis=1), np.concatenate([-s, c], axis=1)], axis=1)
    return jnp.asarray(m.reshape(2 * n1, 2 * n1), dtype=F32).astype(BF16)


def _stage2_dft(s):
    n2 = BLOCK
    n1 = s // n2
    k1 = np.arange(n1)[:, None]
    k2 = np.arange(n2)[:, None]
    n = np.arange(n2)[None, :]
    a1 = 2.0 * np.pi * ((k1 * n) % s) / s
    a2 = 2.0 * np.pi * ((k2 * n) % n2) / n2
    c1, s1 = jnp.asarray(np.cos(a1), F32)[:, None, :], jnp.asarray(np.sin(a1), F32)[:, None, :]
    c2, s2 = jnp.asarray(np.cos(a2), F32)[None], jnp.asarray(np.sin(a2), F32)[None]
    scale = float(1.0 / np.sqrt(float(s) * FOURIER_GROUP_DIM))
    cos = (c1 * c2 - s1 * s2) * scale
    sin = (s1 * c2 + c1 * s2) * scale
    return jnp.stack([cos, sin], axis=3).reshape(n1, n2, 2 * n2).astype(BF16)


def _trunk(x, p, tables):
    b, s, _ = x.shape
    cos, sin, bias, seg, wc, m1, m2 = tables
    for i in range(DEPTH):
        j = i // 2
        if i % 2 == 0:
            q, k, vt = _qkv_call(x, j, p["attn_norm_g"], p["w_qkv"], p["q_norm_g"],
                                 p["k_norm_g"], cos, sin, seg)
            mix = _attn_call(p["attn_sinks"], j, q, k, vt, bias)
            x = _proj_ffn_call(x, mix, j, i, p["w_o_attn"],
                               p["ffn_norm_g"], p["w_gate_up"], p["w_down"])
        else:
            y = _dft1_call(x, j, p["fourier_norm_g"], wc, m1)
            mix = _dft2_call(y, m2, s)
            x = _proj_ffn_call(x, mix, j, i, p["w_fourier_out"],
                               p["ffn_norm_g"], p["w_gate_up"], p["w_down"])
    return x


def kernel(x_prompt, x_sample, attn_norm_g, w_qkv, q_norm_g, k_norm_g, attn_sinks, w_o_attn,
           fourier_norm_g, w_fourier_out, ffn_norm_g, w_gate_up, w_down):
    p = {
        "attn_norm_g": attn_norm_g[:, None, :],
        "w_qkv": w_qkv.astype(BF16),
        "q_norm_g": jnp.tile(q_norm_g, (1, N_Q_HEADS))[:, None, :],
        "k_norm_g": jnp.tile(k_norm_g, (1, N_KV_HEADS))[:, None, :],
        "attn_sinks": attn_sinks,
        "w_o_attn": w_o_attn.astype(BF16),
        "fourier_norm_g": fourier_norm_g[:, None, :],
        "w_fourier_out": w_fourier_out.astype(BF16),
        "ffn_norm_g": ffn_norm_g[:, None, :],
        "w_gate_up": w_gate_up.astype(BF16),
        "w_down": w_down.astype(BF16),
    }
    bias, seg, wc = _band_bias(), _segment_ones(), _channel_dft()
    outs = []
    for x in (x_prompt, x_sample):
        s = x.shape[1]
        cos, sin = _rope_tables(s)
        tables = (cos, sin, bias, seg, wc, _stage1_dft(s // BLOCK), _stage2_dft(s))
        outs.append(_trunk(x, p, tables))
    return tuple(outs)
```

```python
import functools

import numpy as np
import jax
import jax.numpy as jnp
from jax import lax
from jax.experimental import pallas as pl
from jax.experimental.pallas import tpu as pltpu

D_MODEL = 1024
HEAD_DIM = 64
N_Q_HEADS = 16
N_KV_HEADS = 4
QKV_DIM = (N_Q_HEADS + 2 * N_KV_HEADS) * HEAD_DIM
KV_DIM = N_KV_HEADS * HEAD_DIM
WINDOW = 128
BLOCK = 128
ROPE_THETA = 10000.0
N_FOURIER_GROUPS = 4
FOURIER_GROUP_DIM = D_MODEL // N_FOURIER_GROUPS
D_FF = 2816
EPS = 1e-6
NEG_INF = -1e30
DEPTH = 4

LANES = 128
SUBLANES = 8
MXU_DIM = 256
VMEM_LIMIT_BYTES = 56 * 1024 * 1024

ROW_TILE = 2048
FFN_ROW_TILE = 1024
ATTN_Q_TILE = 1024
QKV_SUB_ROWS = 128
FFN_CHUNK = 256
DFT_STEP = 16
DFT_COLS = 4

LOG2_E = 1.4426950408889634
BF16 = jnp.bfloat16
F32 = jnp.float32


def _resident(shape):
    nd = len(shape)
    return pl.BlockSpec(shape, lambda *_: (0,) * nd, pipeline_mode=pl.Buffered(1))


def _layer(shape, layer):
    nd = len(shape)
    return pl.BlockSpec((None,) + tuple(shape), lambda *_: (layer,) + (0,) * nd,
                        pipeline_mode=pl.Buffered(1))


def _params(n_axes, flags=None):
    return pltpu.CompilerParams(
        dimension_semantics=("arbitrary",) * n_axes,
        vmem_limit_bytes=VMEM_LIMIT_BYTES,
        flags=flags,
    )


def _rms_scale(x):
    return lax.rsqrt(jnp.mean(x * x, axis=-1, keepdims=True) + EPS)


def _head_mean_square(t, seg_ref):
    sq = (t * t).astype(BF16)
    seg = seg_ref[...]
    cols = []
    for c in range(t.shape[1] // MXU_DIM):
        sl = slice(c * MXU_DIM, (c + 1) * MXU_DIM)
        cols.append(jnp.dot(sq[:, sl], seg, preferred_element_type=F32))
    ss = cols[0] if len(cols) == 1 else jnp.concatenate(cols, axis=1)
    return ss * (1.0 / HEAD_DIM)


def _rope(t, cos, sin_signed):
    half = HEAD_DIM // 2
    lane = lax.broadcasted_iota(jnp.int32, (t.shape[0], LANES), 1)
    first_half = (lane % HEAD_DIM) < half
    outs = []
    for c in range(t.shape[1] // LANES):
        tc = t[:, c * LANES:(c + 1) * LANES]
        fwd = pltpu.roll(tc, LANES - half, axis=1)
        bwd = pltpu.roll(tc, half, axis=1)
        partner = jnp.where(first_half, fwd, bwd)
        outs.append(tc * cos + partner * sin_signed)
    return outs


def _expand_kv(chunks):
    lane = lax.broadcasted_iota(jnp.int32, chunks[0].shape, 1)
    low = lane < HEAD_DIM
    zero = jnp.zeros_like(chunks[0])
    outs = []
    for tc in chunks:
        sw = pltpu.roll(tc, HEAD_DIM, axis=1)
        outs += [jnp.where(low, tc, zero), jnp.where(low, zero, sw),
                 jnp.where(low, sw, zero), jnp.where(low, zero, tc)]
    return jnp.concatenate(outs, axis=1)


def _qkv_kernel(x_ref, g_ref, w_ref, gq_ref, gk_ref, cos_ref, sin_ref, seg_ref,
                q_ref, k_ref, vt_ref):
    scale = HEAD_DIM ** -0.5 * LOG2_E
    for r in range(x_ref.shape[0] // QKV_SUB_ROWS):
        rows = slice(r * QKV_SUB_ROWS, (r + 1) * QKV_SUB_ROWS)
        x = x_ref[rows, :]
        xn = (x * _rms_scale(x) * g_ref[...]).astype(BF16)
        qkv = jnp.dot(xn, w_ref[...], preferred_element_type=F32)
        q = qkv[:, :D_MODEL]
        k = qkv[:, D_MODEL:D_MODEL + KV_DIM]
        v = qkv[:, D_MODEL + KV_DIM:]
        cos = cos_ref[rows, :]
        sin = sin_ref[rows, :]
        qn = q * lax.rsqrt(_head_mean_square(q, seg_ref) + EPS) * gq_ref[...]
        kn = k * lax.rsqrt(_head_mean_square(k, seg_ref) + EPS) * gk_ref[...]
        q_ref[rows, :] = (jnp.concatenate(_rope(qn, cos, sin), axis=1) * scale).astype(BF16)
        k_ref[rows, :] = _expand_kv(_rope(kn, cos, sin)).astype(BF16)
        vt_ref[:, rows] = v.T.astype(BF16)


def _qkv_call(x, layer, g, w, gq, gk, cos, sin, seg):
    b, s, _ = x.shape
    tm = min(ROW_TILE, s)
    row = lambda width: pl.BlockSpec((None, tm, width), lambda bi, i: (bi, i, 0))
    tab = pl.BlockSpec((tm, LANES), lambda bi, i: (i, 0))
    out = jax.ShapeDtypeStruct((b, s, D_MODEL), BF16)
    return pl.pallas_call(
        _qkv_kernel,
        out_shape=(out, out, jax.ShapeDtypeStruct((b, KV_DIM, s), BF16)),
        grid=(b, s // tm),
        in_specs=[row(D_MODEL), _layer((1, D_MODEL), layer), _layer((D_MODEL, QKV_DIM), layer),
                  _layer((1, D_MODEL), layer), _layer((1, KV_DIM), layer), tab, tab,
                  _resident((MXU_DIM, MXU_DIM))],
        out_specs=(row(D_MODEL), row(D_MODEL),
                   pl.BlockSpec((None, KV_DIM, tm), lambda bi, i: (bi, 0, i))),
        compiler_params=_params(2),
        name="qkv_rope",
    )(x, g, w, gq, gk, cos, sin, seg)


def _attn_kernel(sink_ref, q_ref, kp_ref, kc_ref, kn_ref, vp_ref, vc_ref, vn_ref, bias_ref,
                 o_ref, *, layer):
    tq = q_ref.shape[0]
    i = pl.program_id(1)
    last = pl.num_programs(1) - 1
    n_sub = tq // BLOCK
    n_stage = n_sub * N_KV_HEADS
    keys = 3 * BLOCK
    row = lax.broadcasted_iota(jnp.int32, (LANES, 2 * BLOCK), 0)
    top = row < HEAD_DIM
    vzero = jnp.zeros((HEAD_DIM, keys), BF16)
    ones_row = lax.broadcasted_iota(jnp.int32, (2 * SUBLANES, 2 * keys), 0)
    ones_col = lax.broadcasted_iota(jnp.int32, (2 * SUBLANES, 2 * keys), 1)
    ones_rows = jnp.where((ones_row == 0) & (ones_col < keys) | (ones_row == 1) & (ones_col >= keys),
                          1.0, 0.0).astype(BF16)

    def key_rows(j, lanes):
        parts = []
        if j == 0:
            parts.append(kp_ref[:, lanes])
        parts.append(kc_ref[max(j - 1, 0) * BLOCK:min(j + 2, n_sub) * BLOCK, lanes])
        if j == n_sub - 1:
            parts.append(kn_ref[:, lanes])
        return jnp.concatenate(parts, axis=0)

    def value_cols(j, rows):
        parts = []
        if j == 0:
            parts.append(vp_ref[rows, :])
        parts.append(vc_ref[rows, max(j - 1, 0) * BLOCK:min(j + 2, n_sub) * BLOCK])
        if j == n_sub - 1:
            parts.append(vn_ref[rows, :])
        return jnp.concatenate(parts, axis=1)

    def bias_for(j):
        b0 = bias_ref[0]
        if j == 0:
            b0 = jnp.where(i == 0, bias_ref[1], b0)
        if j == n_sub - 1:
            b0 = jnp.where(i == last, bias_ref[2], b0)
        return b0

    def scores(t):
        j, h = divmod(t, N_KV_HEADS)
        c0 = h * 2 * LANES
        qrows = slice(j * BLOCK, (j + 1) * BLOCK)
        q2 = jnp.concatenate([q_ref[qrows, c0:c0 + LANES],
                              q_ref[qrows, c0 + LANES:c0 + 2 * LANES]], axis=0)
        kab = jnp.concatenate([key_rows(j, slice(c0, c0 + LANES)),
                               key_rows(j, slice(c0 + LANES, c0 + 2 * LANES))], axis=0)
        return lax.dot_general(kab, q2, (((1,), (1,)), ((), ())), preferred_element_type=F32)

    st_next = scores(0)
    bias = None
    for t in range(n_stage):
        j, h = divmod(t, N_KV_HEADS)
        if h == 0:
            bias = bias_for(j)
        c0 = h * 2 * LANES
        qrows = slice(j * BLOCK, (j + 1) * BLOCK)
        st = st_next
        if t + 1 < n_stage:
            st_next = scores(t + 1)
        vt = value_cols(j, slice(h * HEAD_DIM, (h + 1) * HEAD_DIM))
        vabt = jnp.concatenate([jnp.concatenate([vt, vzero], axis=1),
                                jnp.concatenate([vzero, vt], axis=1),
                                ones_rows], axis=0)
        p_rows = []
        e_rows = []
        for ab in range(2):
            p_cols = []
            e_cols = []
            for ch in range(2):
                sink = sink_ref[layer, 4 * h + 2 * ch + ab] * LOG2_E
                sc = st[ab * keys:(ab + 1) * keys, ch * BLOCK:(ch + 1) * BLOCK]
                sc = jnp.concatenate([sc[:BLOCK] + bias[:BLOCK], sc[BLOCK:2 * BLOCK],
                                      sc[2 * BLOCK:] + bias[2 * BLOCK:]], axis=0)
                m = jnp.maximum(jnp.max(sc, axis=0, keepdims=True), sink)
                e_cols.append(jnp.exp2(sink - m))
                p_cols.append(jnp.exp2(sc - m).astype(BF16))
            p_rows.append(jnp.concatenate(p_cols, axis=1))
            e_rows.append(jnp.concatenate(e_cols, axis=1))
        pt = jnp.concatenate(p_rows, axis=0)
        o2t = jnp.dot(vabt, pt, preferred_element_type=F32)
        inv_a = 1.0 / (o2t[LANES:LANES + 1] + e_rows[0])
        inv_b = 1.0 / (o2t[LANES + 1:LANES + 2] + e_rows[1])
        o2 = (o2t[:LANES] * jnp.where(top, inv_a, inv_b)).T
        o_ref[qrows, c0:c0 + LANES] = o2[:BLOCK].astype(BF16)
        o_ref[qrows, c0 + LANES:c0 + 2 * LANES] = o2[BLOCK:].astype(BF16)


def _attn_call(sinks, layer, q, k, vt, bias_t):
    b, s, _ = q.shape
    tq = min(ATTN_Q_TILE, s)
    r = tq // BLOCK
    nb = s // BLOCK
    assert s % tq == 0 and s >= 2 * BLOCK
    prev_i = lambda i: jnp.maximum(i * r - 1, 0)
    next_i = lambda i: jnp.minimum((i + 1) * r, nb - 1)
    cur = pl.BlockSpec((None, tq, D_MODEL), lambda bi, i: (bi, i, 0))
    prev = pl.BlockSpec((None, BLOCK, D_MODEL), lambda bi, i: (bi, prev_i(i), 0))
    nxt = pl.BlockSpec((None, BLOCK, D_MODEL), lambda bi, i: (bi, next_i(i), 0))
    vcur = pl.BlockSpec((None, KV_DIM, tq), lambda bi, i: (bi, 0, i))
    vprev = pl.BlockSpec((None, KV_DIM, BLOCK), lambda bi, i: (bi, 0, prev_i(i)))
    vnxt = pl.BlockSpec((None, KV_DIM, BLOCK), lambda bi, i: (bi, 0, next_i(i)))
    return pl.pallas_call(
        functools.partial(_attn_kernel, layer=layer),
        out_shape=jax.ShapeDtypeStruct((b, s, D_MODEL), BF16),
        grid=(b, s // tq),
        in_specs=[pl.BlockSpec(memory_space=pltpu.SMEM), cur, prev, cur, nxt, vprev, vcur, vnxt,
                  _resident((3, 3 * BLOCK, BLOCK))],
        out_specs=cur,
        compiler_params=_params(2),
        name="band_attention",
    )(sinks, q, k, k, k, vt, vt, vt, bias_t)


def _proj_ffn_rows(x, mix, wp_ref, g_ref, wgu_ref, wd_ref):
    x1 = x + jnp.dot(mix.astype(BF16), wp_ref[...], preferred_element_type=F32)
    xn = (x1 * _rms_scale(x1) * g_ref[...]).astype(BF16)
    acc = x1
    for c in range(D_FF // FFN_CHUNK):
        lo = c * FFN_CHUNK
        gate = jnp.dot(xn, wgu_ref[:, lo:lo + FFN_CHUNK], preferred_element_type=F32)
        up = jnp.dot(xn, wgu_ref[:, D_FF + lo:D_FF + lo + FFN_CHUNK], preferred_element_type=F32)
        hid = (gate * jax.nn.sigmoid(gate) * up).astype(BF16)
        acc = acc + jnp.dot(hid, wd_ref[lo:lo + FFN_CHUNK, :], preferred_element_type=F32)
    return acc


def _ffn_weight_specs(mixer_layer, layer):
    return [_layer((D_MODEL, D_MODEL), mixer_layer), _layer((1, D_MODEL), layer),
            _layer((D_MODEL, 2 * D_FF), layer), _layer((D_FF, D_MODEL), layer)]


def _proj_ffn_kernel(x_ref, mix_ref, wp_ref, g_ref, wgu_ref, wd_ref, o_ref):
    o_ref[...] = _proj_ffn_rows(x_ref[...], mix_ref[...], wp_ref, g_ref, wgu_ref, wd_ref)


def _proj_ffn_call(x, mix, mixer_layer, layer, wp, g, wgu, wd):
    b, s, _ = x.shape
    t = b * s
    tm = min(FFN_ROW_TILE, t)
    row = pl.BlockSpec((tm, D_MODEL), lambda i: (i, 0))
    return pl.pallas_call(
        _proj_ffn_kernel,
        out_shape=jax.ShapeDtypeStruct((t, D_MODEL), F32),
        grid=(t // tm,),
        in_specs=[row, row] + _ffn_weight_specs(mixer_layer, layer),
        out_specs=row,
        compiler_params=_params(1),
        name="proj_ffn",
    )(x.reshape(t, D_MODEL), mix.reshape(t, D_MODEL), wp, g, wgu, wd).reshape(b, s, D_MODEL)


def _dft1_kernel(x_ref, g_ref, wc_ref, m1_ref, y_ref):
    n1 = x_ref.shape[0]
    gd = FOURIER_GROUP_DIM
    for h in range(DFT_STEP // DFT_COLS):
        js = range(h * DFT_COLS, (h + 1) * DFT_COLS)
        xs = jnp.concatenate([x_ref[:, j, :] for j in js], axis=0)
        xn = (xs * _rms_scale(xs) * g_ref[...]).astype(BF16)
        z = [jnp.dot(xn[:, gi * gd:(gi + 1) * gd], wc_ref[...], preferred_element_type=F32)
             for gi in range(N_FOURIER_GROUPS)]
        zr = jnp.concatenate([zg[:, :gd] for zg in z], axis=1).astype(BF16)
        zi = jnp.concatenate([zg[:, gd:] for zg in z], axis=1).astype(BF16)
        rhs = jnp.concatenate(
            [jnp.concatenate([zr[jj * n1:(jj + 1) * n1], zi[jj * n1:(jj + 1) * n1]], axis=0)
             for jj in range(DFT_COLS)], axis=1)
        y = jnp.dot(m1_ref[...], rhs, preferred_element_type=F32).astype(BF16)
        yw = pltpu.bitcast(y, jnp.uint32)
        for jj, j in enumerate(js):
            y_ref[j] = yw[:, jj * D_MODEL:(jj + 1) * D_MODEL]


def _dft1_call(x, layer, g, wc, m1):
    b, s, _ = x.shape
    n2 = BLOCK
    n1 = s // n2
    return pl.pallas_call(
        _dft1_kernel,
        out_shape=jax.ShapeDtypeStruct((b, n2, n1, D_MODEL), jnp.uint32),
        grid=(b, n2 // DFT_STEP),
        in_specs=[pl.BlockSpec((None, n1, DFT_STEP, D_MODEL), lambda bi, i: (bi, 0, i, 0)),
                  _layer((1, D_MODEL), layer),
                  _resident((FOURIER_GROUP_DIM, 2 * FOURIER_GROUP_DIM)),
                  _resident((2 * n1, 2 * n1))],
        out_specs=pl.BlockSpec((None, DFT_STEP, n1, D_MODEL), lambda bi, i: (bi, i, 0, 0)),
        compiler_params=_params(2),
        name="dft_stage1",
    )(x.reshape(b, n1, n2, D_MODEL), g, wc, m1)


def _dft2_kernel(y_ref, m2_ref, o_ref):
    for j in range(DFT_STEP):
        rhs = pltpu.bitcast(y_ref[:, j, :], BF16)
        o_ref[:, j, :] = jnp.dot(m2_ref[j], rhs, preferred_element_type=F32)


def _dft2_call(y, m2, s):
    b = y.shape[0]
    n2 = BLOCK
    n1 = s // n2
    blk = pl.BlockSpec((None, n2, DFT_STEP, D_MODEL), lambda bi, i: (bi, 0, i, 0))
    return pl.pallas_call(
        _dft2_kernel,
        out_shape=jax.ShapeDtypeStruct((b, n2, n1, D_MODEL), F32),
        grid=(b, n1 // DFT_STEP),
        in_specs=[blk, pl.BlockSpec((DFT_STEP, n2, 2 * n2), lambda bi, i: (i, 0, 0))],
        out_specs=blk,
        compiler_params=_params(2),
        name="dft_stage2",
    )(y, m2).reshape(b, s, D_MODEL)


def _rope_tables(s):
    half = HEAD_DIM // 2
    inv_freq = ROPE_THETA ** (-jnp.arange(half, dtype=F32) / half)
    ang = jnp.arange(s).astype(F32)[:, None] * inv_freq[None, :]
    cos = jnp.cos(ang)
    sin = jnp.sin(ang)
    reps = LANES // HEAD_DIM
    cos_t = jnp.tile(jnp.concatenate([cos, cos], axis=1), (1, reps))
    sin_t = jnp.tile(jnp.concatenate([-sin, sin], axis=1), (1, reps))
    return cos_t, sin_t


def _band_bias():
    qi = np.arange(BLOCK)[:, None]
    kj = np.arange(3 * BLOCK)[None, :]
    band = np.abs(kj - BLOCK - qi) <= WINDOW
    mid = np.where(band, 0.0, NEG_INF)
    first = np.where(band & (kj >= BLOCK), 0.0, NEG_INF)
    last = np.where(band & (kj < 2 * BLOCK), 0.0, NEG_INF)
    return jnp.asarray(np.stack([mid.T, first.T, last.T]), dtype=F32)


def _segment_ones():
    idx = np.arange(MXU_DIM) // HEAD_DIM
    return jnp.asarray(idx[:, None] == idx[None, :], dtype=BF16)


def _channel_dft():
    c = np.arange(FOURIER_GROUP_DIM)
    ang = 2.0 * np.pi * ((c[:, None] * c[None, :]) % FOURIER_GROUP_DIM) / FOURIER_GROUP_DIM
    return jnp.asarray(np.concatenate([np.cos(ang), -np.sin(ang)], axis=1), dtype=F32).astype(BF16)


def _stage1_dft(n1):
    k = np.arange(n1)
    ang = 2.0 * np.pi * ((k[:, None] * k[None, :]) % n1) / n1
    c, s = np.cos(ang), np.sin(ang)
    m = np.stack([np.concatenate([c, s], axis=1), np.concatenate([-s, c], axis=1)], axis=1)
    return jnp.asarray(m.reshape(2 * n1, 2 * n1), dtype=F32).astype(BF16)


def _stage2_dft(s):
    n2 = BLOCK
    n1 = s // n2
    k1 = np.arange(n1)[:, None]
    k2 = np.arange(n2)[:, None]
    n = np.arange(n2)[None, :]
    a1 = 2.0 * np.pi * ((k1 * n) % s) / s
    a2 = 2.0 * np.pi * ((k2 * n) % n2) / n2
    c1, s1 = jnp.asarray(np.cos(a1), F32)[:, None, :], jnp.asarray(np.sin(a1), F32)[:, None, :]
    c2, s2 = jnp.asarray(np.cos(a2), F32)[None], jnp.asarray(np.sin(a2), F32)[None]
    scale = float(1.0 / np.sqrt(float(s) * FOURIER_GROUP_DIM))
    cos = (c1 * c2 - s1 * s2) * scale
    sin = (s1 * c2 + c1 * s2) * scale
    return jnp.stack([cos, sin], axis=3).reshape(n1, n2, 2 * n2).astype(BF16)


def _trunk(x, p, tables):
    b, s, _ = x.shape
    cos, sin, bias, seg, wc, m1, m2 = tables
    for i in range(DEPTH):
        j = i // 2
        if i % 2 == 0:
            q, k, vt = _qkv_call(x, j, p["attn_norm_g"], p["w_qkv"], p["q_norm_g"],
                                 p["k_norm_g"], cos, sin, seg)
            mix = _attn_call(p["attn_sinks"], j, q, k, vt, bias)
            x = _proj_ffn_call(x, mix, j, i, p["w_o_attn"],
                               p["ffn_norm_g"], p["w_gate_up"], p["w_down"])
        else:
            y = _dft1_call(x, j, p["fourier_norm_g"], wc, m1)
            mix = _dft2_call(y, m2, s)
            x = _proj_ffn_call(x, mix, j, i, p["w_fourier_out"],
                               p["ffn_norm_g"], p["w_gate_up"], p["w_down"])
    return x


def kernel(x_prompt, x_sample, attn_norm_g, w_qkv, q_norm_g, k_norm_g, attn_sinks, w_o_attn,
           fourier_norm_g, w_fourier_out, ffn_norm_g, w_gate_up, w_down):
    p = {
        "attn_norm_g": attn_norm_g[:, None, :],
        "w_qkv": w_qkv.astype(BF16),
        "q_norm_g": jnp.tile(q_norm_g, (1, N_Q_HEADS))[:, None, :],
        "k_norm_g": jnp.tile(k_norm_g, (1, N_KV_HEADS))[:, None, :],
        "attn_sinks": attn_sinks,
        "w_o_attn": w_o_attn.astype(BF16),
        "fourier_norm_g": fourier_norm_g[:, None, :],
        "w_fourier_out": w_fourier_out.astype(BF16),
        "ffn_norm_g": ffn_norm_g[:, None, :],
        "w_gate_up": w_gate_up.astype(BF16),
        "w_down": w_down.astype(BF16),
    }
    bias, seg, wc = _band_bias(), _segment_ones(), _channel_dft()
    outs = []
    for x in (x_prompt, x_sample):
        s = x.shape[1]
        cos, sin = _rope_tables(s)
        tables = (cos, sin, bias, seg, wc, _stage1_dft(s // BLOCK), _stage2_dft(s))
        outs.append(_trunk(x, p, tables))
    return tuple(outs)
```

```python
import functools

import numpy as np
import jax
import jax.numpy as jnp
from jax import lax
from jax.experimental import pallas as pl
from jax.experimental.pallas import tpu as pltpu

D_MODEL = 1024
HEAD_DIM = 64
N_Q_HEADS = 16
N_KV_HEADS = 4
QKV_DIM = (N_Q_HEADS + 2 * N_KV_HEADS) * HEAD_DIM
KV_DIM = N_KV_HEADS * HEAD_DIM
WINDOW = 128
BLOCK = 128
ROPE_THETA = 10000.0
N_FOURIER_GROUPS = 4
FOURIER_GROUP_DIM = D_MODEL // N_FOURIER_GROUPS
D_FF = 2816
EPS = 1e-6
NEG_INF = -1e30
DEPTH = 4

LANES = 128
SUBLANES = 8
MXU_DIM = 256
VMEM_LIMIT_BYTES = 56 * 1024 * 1024

QKV_ROW_TILE = 1024
FFN_ROW_TILE = 1024
ATTN_Q_TILE = 1024
QKV_SUB_ROWS = 128
FFN_CHUNK = 256
DFT_STEP = 16
DFT_COLS = 4

LOG2_E = 1.4426950408889634
BF16 = jnp.bfloat16
F32 = jnp.float32


def _resident(shape):
    nd = len(shape)
    return pl.BlockSpec(shape, lambda *_: (0,) * nd, pipeline_mode=pl.Buffered(1))


def _layer(shape, layer):
    nd = len(shape)
    return pl.BlockSpec((None,) + tuple(shape), lambda *_: (layer,) + (0,) * nd,
                        pipeline_mode=pl.Buffered(1))


def _params(n_axes):
    return pltpu.CompilerParams(
        dimension_semantics=("arbitrary",) * n_axes,
        vmem_limit_bytes=VMEM_LIMIT_BYTES,
    )


def _rms_scale(x):
    return lax.rsqrt(jnp.mean(x * x, axis=-1, keepdims=True) + EPS)


def _head_mean_square(t, seg_ref):
    sq = (t * t).astype(BF16)
    seg = seg_ref[...]
    cols = []
    for c in range(t.shape[1] // MXU_DIM):
        sl = slice(c * MXU_DIM, (c + 1) * MXU_DIM)
        cols.append(jnp.dot(sq[:, sl], seg, preferred_element_type=F32))
    ss = cols[0] if len(cols) == 1 else jnp.concatenate(cols, axis=1)
    return ss * (1.0 / HEAD_DIM)


def _rope(t, cos, sin_signed):
    half = HEAD_DIM // 2
    lane = lax.broadcasted_iota(jnp.int32, (t.shape[0], LANES), 1)
    first_half = (lane % HEAD_DIM) < half
    outs = []
    for c in range(t.shape[1] // LANES):
        tc = t[:, c * LANES:(c + 1) * LANES]
        fwd = pltpu.roll(tc, LANES - half, axis=1)
        bwd = pltpu.roll(tc, half, axis=1)
        partner = jnp.where(first_half, fwd, bwd)
        outs.append(tc * cos + partner * sin_signed)
    return outs


def _expand_kv(chunks):
    lane = lax.broadcasted_iota(jnp.int32, chunks[0].shape, 1)
    low = lane < HEAD_DIM
    zero = jnp.zeros_like(chunks[0])
    outs = []
    for tc in chunks:
        sw = pltpu.roll(tc, HEAD_DIM, axis=1)
        outs += [jnp.where(low, tc, zero), jnp.where(low, zero, sw),
                 jnp.where(low, sw, zero), jnp.where(low, zero, tc)]
    return jnp.concatenate(outs, axis=1)


def _qkv_kernel(x_ref, g_ref, w_ref, gq_ref, gk_ref, cos_ref, sin_ref, seg_ref,
                q_ref, k_ref, vt_ref):
    scale = HEAD_DIM ** -0.5 * LOG2_E
    for r in range(x_ref.shape[0] // QKV_SUB_ROWS):
        rows = slice(r * QKV_SUB_ROWS, (r + 1) * QKV_SUB_ROWS)
        x = x_ref[rows, :]
        xn = (x * _rms_scale(x) * g_ref[...]).astype(BF16)
        qkv = jnp.dot(xn, w_ref[...], preferred_element_type=F32)
        q = qkv[:, :D_MODEL]
        k = qkv[:, D_MODEL:D_MODEL + KV_DIM]
        v = qkv[:, D_MODEL + KV_DIM:]
        cos = cos_ref[rows, :]
        sin = sin_ref[rows, :]
        qn = q * lax.rsqrt(_head_mean_square(q, seg_ref) + EPS) * gq_ref[...]
        kn = k * lax.rsqrt(_head_mean_square(k, seg_ref) + EPS) * gk_ref[...]
        q_ref[rows, :] = (jnp.concatenate(_rope(qn, cos, sin), axis=1) * scale).astype(BF16)
        k_ref[rows, :] = _expand_kv(_rope(kn, cos, sin)).astype(BF16)
        vt_ref[:, rows] = v.T.astype(BF16)


def _qkv_call(x, layer, g, w, gq, gk, cos, sin, seg):
    b, s, _ = x.shape
    tm = min(QKV_ROW_TILE, s)
    assert s % tm == 0 and tm % QKV_SUB_ROWS == 0
    row = lambda width: pl.BlockSpec((None, tm, width), lambda bi, i: (bi, i, 0))
    tab = pl.BlockSpec((tm, LANES), lambda bi, i: (i, 0))
    out = jax.ShapeDtypeStruct((b, s, D_MODEL), BF16)
    return pl.pallas_call(
        _qkv_kernel,
        out_shape=(out, out, jax.ShapeDtypeStruct((b, KV_DIM, s), BF16)),
        grid=(b, s // tm),
        in_specs=[row(D_MODEL), _layer((1, D_MODEL), layer), _layer((D_MODEL, QKV_DIM), layer),
                  _layer((1, D_MODEL), layer), _layer((1, KV_DIM), layer), tab, tab,
                  _resident((MXU_DIM, MXU_DIM))],
        out_specs=(row(D_MODEL), row(D_MODEL),
                   pl.BlockSpec((None, KV_DIM, tm), lambda bi, i: (bi, 0, i))),
        compiler_params=_params(2),
        name="qkv_rope",
    )(x, g, w, gq, gk, cos, sin, seg)


def _attn_kernel(sink_ref, q_ref, kp_ref, kc_ref, kn_ref, vp_ref, vc_ref, vn_ref, bias_ref,
                 o_ref, *, layer):
    tq = q_ref.shape[0]
    i = pl.program_id(1)
    last = pl.num_programs(1) - 1
    n_sub = tq // BLOCK
    n_stage = n_sub * N_KV_HEADS
    keys = 3 * BLOCK
    row = lax.broadcasted_iota(jnp.int32, (LANES, 2 * BLOCK), 0)
    top = row < HEAD_DIM
    vzero = jnp.zeros((HEAD_DIM, keys), BF16)
    ones_row = lax.broadcasted_iota(jnp.int32, (2 * SUBLANES, 2 * keys), 0)
    ones_col = lax.broadcasted_iota(jnp.int32, (2 * SUBLANES, 2 * keys), 1)
    ones_rows = jnp.where((ones_row == 0) & (ones_col < keys) | (ones_row == 1) & (ones_col >= keys),
                          1.0, 0.0).astype(BF16)

    def key_rows(j, lanes):
        parts = []
        if j == 0:
            parts.append(kp_ref[:, lanes])
        parts.append(kc_ref[max(j - 1, 0) * BLOCK:min(j + 2, n_sub) * BLOCK, lanes])
        if j == n_sub - 1:
            parts.append(kn_ref[:, lanes])
        return jnp.concatenate(parts, axis=0)

    def value_cols(j, rows):
        parts = []
        if j == 0:
            parts.append(vp_ref[rows, :])
        parts.append(vc_ref[rows, max(j - 1, 0) * BLOCK:min(j + 2, n_sub) * BLOCK])
        if j == n_sub - 1:
            parts.append(vn_ref[rows, :])
        return jnp.concatenate(parts, axis=1)

    def bias_for(j):
        b0 = bias_ref[0]
        if j == 0:
            b0 = jnp.where(i == 0, bias_ref[1], b0)
        if j == n_sub - 1:
            b0 = jnp.where(i == last, bias_ref[2], b0)
        return b0

    def scores(t):
        j, h = divmod(t, N_KV_HEADS)
        c0 = h * 2 * LANES
        qrows = slice(j * BLOCK, (j + 1) * BLOCK)
        q2 = jnp.concatenate([q_ref[qrows, c0:c0 + LANES],
                              q_ref[qrows, c0 + LANES:c0 + 2 * LANES]], axis=0)
        kab = jnp.concatenate([key_rows(j, slice(c0, c0 + LANES)),
                               key_rows(j, slice(c0 + LANES, c0 + 2 * LANES))], axis=0)
        return lax.dot_general(kab, q2, (((1,), (1,)), ((), ())), preferred_element_type=F32)

    st_next = scores(0)
    bias = None
    for t in range(n_stage):
        j, h = divmod(t, N_KV_HEADS)
        if h == 0:
            bias = bias_for(j)
        c0 = h * 2 * LANES
        qrows = slice(j * BLOCK, (j + 1) * BLOCK)
        st = st_next
        if t + 1 < n_stage:
            st_next = scores(t + 1)
        vt = value_cols(j, slice(h * HEAD_DIM, (h + 1) * HEAD_DIM))
        vabt = jnp.concatenate([jnp.concatenate([vt, vzero], axis=1),
                                jnp.concatenate([vzero, vt], axis=1),
                                ones_rows], axis=0)
        p_rows = []
        e_rows = []
        for ab in range(2):
            p_cols = []
            e_cols = []
            for ch in range(2):
                sink = sink_ref[layer, 4 * h + 2 * ch + ab] * LOG2_E
                sc = st[ab * keys:(ab + 1) * keys, ch * BLOCK:(ch + 1) * BLOCK]
                sc = jnp.concatenate([sc[:BLOCK] + bias[:BLOCK], sc[BLOCK:2 * BLOCK],
                                      sc[2 * BLOCK:] + bias[2 * BLOCK:]], axis=0)
                m = jnp.maximum(jnp.max(sc, axis=0, keepdims=True), sink)
                e_cols.append(jnp.exp2(sink - m))
                p_cols.append(jnp.exp2(sc - m).astype(BF16))
            p_rows.append(jnp.concatenate(p_cols, axis=1))
            e_rows.append(jnp.concatenate(e_cols, axis=1))
        pt = jnp.concatenate(p_rows, axis=0)
        o2t = jnp.dot(vabt, pt, preferred_element_type=F32)
        inv_a = 1.0 / (o2t[LANES:LANES + 1] + e_rows[0])
        inv_b = 1.0 / (o2t[LANES + 1:LANES + 2] + e_rows[1])
        o2 = (o2t[:LANES] * jnp.where(top, inv_a, inv_b)).T
        o_ref[qrows, c0:c0 + LANES] = o2[:BLOCK].astype(BF16)
        o_ref[qrows, c0 + LANES:c0 + 2 * LANES] = o2[BLOCK:].astype(BF16)


def _attn_call(sinks, layer, q, k, vt, bias_t):
    b, s, _ = q.shape
    tq = min(ATTN_Q_TILE, s)
    r = tq // BLOCK
    nb = s // BLOCK
    assert s % tq == 0 and s >= 2 * BLOCK and WINDOW <= BLOCK
    prev_i = lambda i: jnp.maximum(i * r - 1, 0)
    next_i = lambda i: jnp.minimum((i + 1) * r, nb - 1)
    cur = pl.BlockSpec((None, tq, D_MODEL), lambda bi, i: (bi, i, 0))
    prev = pl.BlockSpec((None, BLOCK, D_MODEL), lambda bi, i: (bi, prev_i(i), 0))
    nxt = pl.BlockSpec((None, BLOCK, D_MODEL), lambda bi, i: (bi, next_i(i), 0))
    vcur = pl.BlockSpec((None, KV_DIM, tq), lambda bi, i: (bi, 0, i))
    vprev = pl.BlockSpec((None, KV_DIM, BLOCK), lambda bi, i: (bi, 0, prev_i(i)))
    vnxt = pl.BlockSpec((None, KV_DIM, BLOCK), lambda bi, i: (bi, 0, next_i(i)))
    return pl.pallas_call(
        functools.partial(_attn_kernel, layer=layer),
        out_shape=jax.ShapeDtypeStruct((b, s, D_MODEL), BF16),
        grid=(b, s // tq),
        in_specs=[pl.BlockSpec(memory_space=pltpu.SMEM), cur, prev, cur, nxt, vprev, vcur, vnxt,
                  _resident((3, 3 * BLOCK, BLOCK))],
        out_specs=cur,
        compiler_params=_params(2),
        name="band_attention",
    )(sinks, q, k, k, k, vt, vt, vt, bias_t)


def _proj_ffn_rows(x, mix, wp_ref, g_ref, wgu_ref, wd_ref):
    x1 = x + jnp.dot(mix.astype(BF16), wp_ref[...], preferred_element_type=F32)
    xn = (x1 * _rms_scale(x1) * g_ref[...]).astype(BF16)
    acc = x1
    for c in range(D_FF // FFN_CHUNK):
        lo = c * FFN_CHUNK
        gate = jnp.dot(xn, wgu_ref[:, lo:lo + FFN_CHUNK], preferred_element_type=F32)
        up = jnp.dot(xn, wgu_ref[:, D_FF + lo:D_FF + lo + FFN_CHUNK], preferred_element_type=F32)
        hid = (gate * jax.nn.sigmoid(gate) * up).astype(BF16)
        acc = acc + jnp.dot(hid, wd_ref[lo:lo + FFN_CHUNK, :], preferred_element_type=F32)
    return acc


def _ffn_weight_specs(mixer_layer, layer):
    return [_layer((D_MODEL, D_MODEL), mixer_layer), _layer((1, D_MODEL), layer),
            _layer((D_MODEL, 2 * D_FF), layer), _layer((D_FF, D_MODEL), layer)]


def _proj_ffn_kernel(x_ref, mix_ref, wp_ref, g_ref, wgu_ref, wd_ref, o_ref):
    o_ref[...] = _proj_ffn_rows(x_ref[...], mix_ref[...], wp_ref, g_ref, wgu_ref, wd_ref)


def _proj_ffn_call(x, mix, mixer_layer, layer, wp, g, wgu, wd):
    b, s, _ = x.shape
    t = b * s
    tm = min(FFN_ROW_TILE, t)
    assert t % tm == 0
    row = pl.BlockSpec((tm, D_MODEL), lambda i: (i, 0))
    return pl.pallas_call(
        _proj_ffn_kernel,
        out_shape=jax.ShapeDtypeStruct((t, D_MODEL), F32),
        grid=(t // tm,),
        in_specs=[row, row] + _ffn_weight_specs(mixer_layer, layer),
        out_specs=row,
        compiler_params=_params(1),
        name="proj_ffn",
    )(x.reshape(t, D_MODEL), mix.reshape(t, D_MODEL), wp, g, wgu, wd).reshape(b, s, D_MODEL)


def _dft1_kernel(x_ref, g_ref, wc_ref, m1_ref, y_ref):
    n1 = x_ref.shape[0]
    gd = FOURIER_GROUP_DIM
    for h in range(DFT_STEP // DFT_COLS):
        js = range(h * DFT_COLS, (h + 1) * DFT_COLS)
        xs = jnp.concatenate([x_ref[:, j, :] for j in js], axis=0)
        xn = (xs * _rms_scale(xs) * g_ref[...]).astype(BF16)
        z = [jnp.dot(xn[:, gi * gd:(gi + 1) * gd], wc_ref[...], preferred_element_type=F32)
             for gi in range(N_FOURIER_GROUPS)]
        zr = jnp.concatenate([zg[:, :gd] for zg in z], axis=1).astype(BF16)
        zi = jnp.concatenate([zg[:, gd:] for zg in z], axis=1).astype(BF16)
        rhs = jnp.concatenate(
            [jnp.concatenate([zr[jj * n1:(jj + 1) * n1], zi[jj * n1:(jj + 1) * n1]], axis=0)
             for jj in range(DFT_COLS)], axis=1)
        y = jnp.dot(m1_ref[...], rhs, preferred_element_type=F32).astype(BF16)
        yw = pltpu.bitcast(y, jnp.uint32)
        for jj, j in enumerate(js):
            y_ref[j] = yw[:, jj * D_MODEL:(jj + 1) * D_MODEL]


def _dft1_call(x, layer, g, wc, m1):
    b, s, _ = x.shape
    n2 = BLOCK
    n1 = s // n2
    assert s % n2 == 0 and n1 % DFT_STEP == 0 and n2 % DFT_STEP == 0
    return pl.pallas_call(
        _dft1_kernel,
        out_shape=jax.ShapeDtypeStruct((b, n2, n1, D_MODEL), jnp.uint32),
        grid=(b, n2 // DFT_STEP),
        in_specs=[pl.BlockSpec((None, n1, DFT_STEP, D_MODEL), lambda bi, i: (bi, 0, i, 0)),
                  _layer((1, D_MODEL), layer),
                  _resident((FOURIER_GROUP_DIM, 2 * FOURIER_GROUP_DIM)),
                  _resident((2 * n1, 2 * n1))],
        out_specs=pl.BlockSpec((None, DFT_STEP, n1, D_MODEL), lambda bi, i: (bi, i, 0, 0)),
        compiler_params=_params(2),
        name="dft_stage1",
    )(x.reshape(b, n1, n2, D_MODEL), g, wc, m1)


def _dft2_kernel(y_ref, m2_ref, o_ref):
    for j in range(DFT_STEP):
        rhs = pltpu.bitcast(y_ref[:, j, :], BF16)
        o_ref[:, j, :] = jnp.dot(m2_ref[j], rhs, preferred_element_type=F32)


def _dft2_call(y, m2, s):
    b = y.shape[0]
    n2 = BLOCK
    n1 = s // n2
    blk = pl.BlockSpec((None, n2, DFT_STEP, D_MODEL), lambda bi, i: (bi, 0, i, 0))
    return pl.pallas_call(
        _dft2_kernel,
        out_shape=jax.ShapeDtypeStruct((b, n2, n1, D_MODEL), F32),
        grid=(b, n1 // DFT_STEP),
        in_specs=[blk, pl.BlockSpec((DFT_STEP, n2, 2 * n2), lambda bi, i: (i, 0, 0))],
        out_specs=blk,
        compiler_params=_params(2),
        name="dft_stage2",
    )(y, m2).reshape(b, s, D_MODEL)


def _rope_tables(s):
    half = HEAD_DIM // 2
    inv_freq = ROPE_THETA ** (-jnp.arange(half, dtype=F32) / half)
    ang = jnp.arange(s).astype(F32)[:, None] * inv_freq[None, :]
    cos = jnp.cos(ang)
    sin = jnp.sin(ang)
    reps = LANES // HEAD_DIM
    cos_t = jnp.tile(jnp.concatenate([cos, cos], axis=1), (1, reps))
    sin_t = jnp.tile(jnp.concatenate([-sin, sin], axis=1), (1, reps))
    return cos_t, sin_t


def _band_bias():
    qi = np.arange(BLOCK)[:, None]
    kj = np.arange(3 * BLOCK)[None, :]
    band = np.abs(kj - BLOCK - qi) <= WINDOW
    mid = np.where(band, 0.0, NEG_INF)
    first = np.where(band & (kj >= BLOCK), 0.0, NEG_INF)
    last = np.where(band & (kj < 2 * BLOCK), 0.0, NEG_INF)
    return jnp.asarray(np.stack([mid.T, first.T, last.T]), dtype=F32)


def _segment_ones():
    idx = np.arange(MXU_DIM) // HEAD_DIM
    return jnp.asarray(idx[:, None] == idx[None, :], dtype=BF16)


def _channel_dft():
    c = np.arange(FOURIER_GROUP_DIM)
    ang = 2.0 * np.pi * ((c[:, None] * c[None, :]) % FOURIER_GROUP_DIM) / FOURIER_GROUP_DIM
    return jnp.asarray(np.concatenate([np.cos(ang), -np.sin(ang)], axis=1), dtype=F32).astype(BF16)


def _stage1_dft(n1):
    k = np.arange(n1)
    ang = 2.0 * np.pi * ((k[:, None] * k[None, :]) % n1) / n1
    c, s = np.cos(ang), np.sin(ang)
    m = np.stack([np.concatenate([c, s], axis=1), np.concatenate([-s, c], axis=1)], axis=1)
    return jnp.asarray(m.reshape(2 * n1, 2 * n1), dtype=F32).astype(BF16)


def _stage2_dft(s):
    n2 = BLOCK
    n1 = s // n2
    k1 = np.arange(n1)[:, None]
    k2 = np.arange(n2)[:, None]
    n = np.arange(n2)[None, :]
    a1 = 2.0 * np.pi * ((k1 * n) % s) / s
    a2 = 2.0 * np.pi * ((k2 * n) % n2) / n2
    c1, s1 = jnp.asarray(np.cos(a1), F32)[:, None, :], jnp.asarray(np.sin(a1), F32)[:, None, :]
    c2, s2 = jnp.asarray(np.cos(a2), F32)[None], jnp.asarray(np.sin(a2), F32)[None]
    scale = float(1.0 / np.sqrt(float(s) * FOURIER_GROUP_DIM))
    cos = (c1 * c2 - s1 * s2) * scale
    sin = (s1 * c2 + c1 * s2) * scale
    return jnp.stack([cos, sin], axis=3).reshape(n1, n2, 2 * n2).astype(BF16)


def _trunk(x, p, tables):
    b, s, _ = x.shape
    cos, sin, bias, seg, wc, m1, m2 = tables
    for i in range(DEPTH):
        j = i // 2
        if i % 2 == 0:
            q, k, vt = _qkv_call(x, j, p["attn_norm_g"], p["w_qkv"], p["q_norm_g"],
                                 p["k_norm_g"], cos, sin, seg)
            mix = _attn_call(p["attn_sinks"], j, q, k, vt, bias)
            x = _proj_ffn_call(x, mix, j, i, p["w_o_attn"],
                               p["ffn_norm_g"], p["w_gate_up"], p["w_down"])
        else:
            y = _dft1_call(x, j, p["fourier_norm_g"], wc, m1)
            mix = _dft2_call(y, m2, s)
            x = _proj_ffn_call(x, mix, j, i, p["w_fourier_out"],
                               p["ffn_norm_g"], p["w_gate_up"], p["w_down"])
    return x


def kernel(x_prompt, x_sample, attn_norm_g, w_qkv, q_norm_g, k_norm_g, attn_sinks, w_o_attn,
           fourier_norm_g, w_fourier_out, ffn_norm_g, w_gate_up, w_down):
    p = {
        "attn_norm_g": attn_norm_g[:, None, :],
        "w_qkv": w_qkv.astype(BF16),
        "q_norm_g": jnp.tile(q_norm_g, (1, N_Q_HEADS))[:, None, :],
        "k_norm_g": jnp.tile(k_norm_g, (1, N_KV_HEADS))[:, None, :],
        "attn_sinks": attn_sinks,
        "w_o_attn": w_o_attn.astype(BF16),
        "fourier_norm_g": fourier_norm_g[:, None, :],
        "w_fourier_out": w_fourier_out.astype(BF16),
        "ffn_norm_g": ffn_norm_g[:, None, :],
        "w_gate_up": w_gate_up.astype(BF16),
        "w_down": w_down.astype(BF16),
    }
    bias, seg, wc = _band_bias(), _segment_ones(), _channel_dft()
    cos, sin = _rope_tables(max(x_prompt.shape[1], x_sample.shape[1]))
    outs = []
    for x in (x_prompt, x_sample):
        s = x.shape[1]
        tables = (cos, sin, bias, seg, wc, _stage1_dft(s // BLOCK), _stage2_dft(s))
        outs.append(_trunk(x, p, tables))
    return tuple(outs)
```

```python
import functools

import numpy as np
import jax
import jax.numpy as jnp
from jax import lax
from jax.experimental import pallas as pl
from jax.experimental.pallas import tpu as pltpu

D_MODEL = 1024
HEAD_DIM = 64
N_Q_HEADS = 16
N_KV_HEADS = 4
QKV_DIM = (N_Q_HEADS + 2 * N_KV_HEADS) * HEAD_DIM
KV_DIM = N_KV_HEADS * HEAD_DIM
WINDOW = 128
BLOCK = 128
ROPE_THETA = 10000.0
N_FOURIER_GROUPS = 4
FOURIER_GROUP_DIM = D_MODEL // N_FOURIER_GROUPS
D_FF = 2816
EPS = 1e-6
NEG_INF = -1e30
DEPTH = 4

LANES = 128
SUBLANES = 8
MXU_DIM = 256
VMEM_LIMIT_BYTES = 56 * 1024 * 1024

QKV_ROW_TILE = 1024
FFN_ROW_TILE = 1024
ATTN_Q_TILE = 2048
QKV_SUB_ROWS = 128
FFN_CHUNK = 256
DFT_STEP = 16
DFT_COLS = 4

LOG2_E = 1.4426950408889634
BF16 = jnp.bfloat16
F32 = jnp.float32


def _resident(shape):
    nd = len(shape)
    return pl.BlockSpec(shape, lambda *_: (0,) * nd, pipeline_mode=pl.Buffered(1))


def _layer(shape, layer):
    nd = len(shape)
    return pl.BlockSpec((None,) + tuple(shape), lambda *_: (layer,) + (0,) * nd,
                        pipeline_mode=pl.Buffered(1))


def _params(n_axes):
    return pltpu.CompilerParams(
        dimension_semantics=("arbitrary",) * n_axes,
        vmem_limit_bytes=VMEM_LIMIT_BYTES,
    )


def _rms_scale(x):
    return lax.rsqrt(jnp.mean(x * x, axis=-1, keepdims=True) + EPS)


def _head_mean_square(t, seg_ref):
    sq = (t * t).astype(BF16)
    seg = seg_ref[...]
    cols = []
    for c in range(t.shape[1] // MXU_DIM):
        sl = slice(c * MXU_DIM, (c + 1) * MXU_DIM)
        cols.append(jnp.dot(sq[:, sl], seg, preferred_element_type=F32))
    ss = cols[0] if len(cols) == 1 else jnp.concatenate(cols, axis=1)
    return ss * (1.0 / HEAD_DIM)


def _rope(t, cos, sin_signed):
    half = HEAD_DIM // 2
    lane = lax.broadcasted_iota(jnp.int32, (t.shape[0], LANES), 1)
    first_half = (lane % HEAD_DIM) < half
    outs = []
    for c in range(t.shape[1] // LANES):
        tc = t[:, c * LANES:(c + 1) * LANES]
        fwd = pltpu.roll(tc, LANES - half, axis=1)
        bwd = pltpu.roll(tc, half, axis=1)
        partner = jnp.where(first_half, fwd, bwd)
        outs.append(tc * cos + partner * sin_signed)
    return outs


def _expand_kv(chunks):
    lane = lax.broadcasted_iota(jnp.int32, chunks[0].shape, 1)
    low = lane < HEAD_DIM
    zero = jnp.zeros_like(chunks[0])
    outs = []
    for tc in chunks:
        sw = pltpu.roll(tc, HEAD_DIM, axis=1)
        outs += [jnp.where(low, tc, zero), jnp.where(low, zero, sw),
                 jnp.where(low, sw, zero), jnp.where(low, zero, tc)]
    return jnp.concatenate(outs, axis=1)


def _qkv_kernel(x_ref, g_ref, w_ref, gq_ref, gk_ref, cos_ref, sin_ref, seg_ref,
                q_ref, k_ref, vt_ref):
    scale = HEAD_DIM ** -0.5 * LOG2_E
    for r in range(x_ref.shape[0] // QKV_SUB_ROWS):
        rows = slice(r * QKV_SUB_ROWS, (r + 1) * QKV_SUB_ROWS)
        x = x_ref[rows, :]
        xn = (x * _rms_scale(x) * g_ref[...]).astype(BF16)
        qkv = jnp.dot(xn, w_ref[...], preferred_element_type=F32)
        q = qkv[:, :D_MODEL]
        k = qkv[:, D_MODEL:D_MODEL + KV_DIM]
        v = qkv[:, D_MODEL + KV_DIM:]
        cos = cos_ref[rows, :]
        sin = sin_ref[rows, :]
        qn = q * lax.rsqrt(_head_mean_square(q, seg_ref) + EPS) * gq_ref[...]
        kn = k * lax.rsqrt(_head_mean_square(k, seg_ref) + EPS) * gk_ref[...]
        q_ref[rows, :] = (jnp.concatenate(_rope(qn, cos, sin), axis=1) * scale).astype(BF16)
        k_ref[rows, :] = _expand_kv(_rope(kn, cos, sin)).astype(BF16)
        vt_ref[:, rows] = v.T.astype(BF16)


def _qkv_call(x, layer, g, w, gq, gk, cos, sin, seg):
    b, s, _ = x.shape
    tm = min(QKV_ROW_TILE, s)
    assert s % tm == 0 and tm % QKV_SUB_ROWS == 0
    row = lambda width: pl.BlockSpec((None, tm, width), lambda bi, i: (bi, i, 0))
    tab = pl.BlockSpec((tm, LANES), lambda bi, i: (i, 0))
    out = jax.ShapeDtypeStruct((b, s, D_MODEL), BF16)
    return pl.pallas_call(
        _qkv_kernel,
        out_shape=(out, out, jax.ShapeDtypeStruct((b, KV_DIM, s), BF16)),
        grid=(b, s // tm),
        in_specs=[row(D_MODEL), _layer((1, D_MODEL), layer), _layer((D_MODEL, QKV_DIM), layer),
                  _layer((1, D_MODEL), layer), _layer((1, KV_DIM), layer), tab, tab,
                  _resident((MXU_DIM, MXU_DIM))],
        out_specs=(row(D_MODEL), row(D_MODEL),
                   pl.BlockSpec((None, KV_DIM, tm), lambda bi, i: (bi, 0, i))),
        compiler_params=_params(2),
        name="qkv_rope",
    )(x, g, w, gq, gk, cos, sin, seg)


def _attn_kernel(sink_ref, q_ref, kp_ref, kc_ref, kn_ref, vp_ref, vc_ref, vn_ref, bias_ref,
                 o_ref, *, layer):
    tq = q_ref.shape[0]
    i = pl.program_id(1)
    last = pl.num_programs(1) - 1
    n_sub = tq // BLOCK
    n_stage = n_sub * N_KV_HEADS
    keys = 3 * BLOCK
    row = lax.broadcasted_iota(jnp.int32, (LANES, 2 * BLOCK), 0)
    top = row < HEAD_DIM
    vzero = jnp.zeros((HEAD_DIM, keys), BF16)
    ones_row = lax.broadcasted_iota(jnp.int32, (2 * SUBLANES, 2 * keys), 0)
    ones_col = lax.broadcasted_iota(jnp.int32, (2 * SUBLANES, 2 * keys), 1)
    ones_rows = jnp.where((ones_row == 0) & (ones_col < keys) | (ones_row == 1) & (ones_col >= keys),
                          1.0, 0.0).astype(BF16)

    def key_rows(j, lanes):
        parts = []
        if j == 0:
            parts.append(kp_ref[:, lanes])
        parts.append(kc_ref[max(j - 1, 0) * BLOCK:min(j + 2, n_sub) * BLOCK, lanes])
        if j == n_sub - 1:
            parts.append(kn_ref[:, lanes])
        return jnp.concatenate(parts, axis=0)

    def value_cols(j, rows):
        parts = []
        if j == 0:
            parts.append(vp_ref[rows, :])
        parts.append(vc_ref[rows, max(j - 1, 0) * BLOCK:min(j + 2, n_sub) * BLOCK])
        if j == n_sub - 1:
            parts.append(vn_ref[rows, :])
        return jnp.concatenate(parts, axis=1)

    def bias_for(j):
        b0 = bias_ref[0]
        if j == 0:
            b0 = jnp.where(i == 0, bias_ref[1], b0)
        if j == n_sub - 1:
            b0 = jnp.where(i == last, bias_ref[2], b0)
        return b0

    def scores(t):
        j, h = divmod(t, N_KV_HEADS)
        c0 = h * 2 * LANES
        qrows = slice(j * BLOCK, (j + 1) * BLOCK)
        q2 = jnp.concatenate([q_ref[qrows, c0:c0 + LANES],
                              q_ref[qrows, c0 + LANES:c0 + 2 * LANES]], axis=0)
        kab = jnp.concatenate([key_rows(j, slice(c0, c0 + LANES)),
                               key_rows(j, slice(c0 + LANES, c0 + 2 * LANES))], axis=0)
        return lax.dot_general(kab, q2, (((1,), (1,)), ((), ())), preferred_element_type=F32)

    st_next = scores(0)
    bias = None
    for t in range(n_stage):
        j, h = divmod(t, N_KV_HEADS)
        if h == 0:
            bias = bias_for(j)
        c0 = h * 2 * LANES
        qrows = slice(j * BLOCK, (j + 1) * BLOCK)
        st = st_next
        if t + 1 < n_stage:
            st_next = scores(t + 1)
        vt = value_cols(j, slice(h * HEAD_DIM, (h + 1) * HEAD_DIM))
        vabt = jnp.concatenate([jnp.concatenate([vt, vzero], axis=1),
                                jnp.concatenate([vzero, vt], axis=1),
                                ones_rows], axis=0)
        p_rows = []
        e_rows = []
        for ab in range(2):
            p_cols = []
            e_cols = []
            for ch in range(2):
                sink = sink_ref[layer, 4 * h + 2 * ch + ab] * LOG2_E
                sc = st[ab * keys:(ab + 1) * keys, ch * BLOCK:(ch + 1) * BLOCK]
                sc = jnp.concatenate([sc[:BLOCK] + bias[:BLOCK], sc[BLOCK:2 * BLOCK],
                                      sc[2 * BLOCK:] + bias[2 * BLOCK:]], axis=0)
                m = jnp.maximum(jnp.max(sc, axis=0, keepdims=True), sink)
                e_cols.append(jnp.exp2(sink - m))
                p_cols.append(jnp.exp2(sc - m).astype(BF16))
            p_rows.append(jnp.concatenate(p_cols, axis=1))
            e_rows.append(jnp.concatenate(e_cols, axis=1))
        pt = jnp.concatenate(p_rows, axis=0)
        o2t = jnp.dot(vabt, pt, preferred_element_type=F32)
        inv_a = 1.0 / (o2t[LANES:LANES + 1] + e_rows[0])
        inv_b = 1.0 / (o2t[LANES + 1:LANES + 2] + e_rows[1])
        o2 = (o2t[:LANES] * jnp.where(top, inv_a, inv_b)).T
        o_ref[qrows, c0:c0 + LANES] = o2[:BLOCK].astype(BF16)
        o_ref[qrows, c0 + LANES:c0 + 2 * LANES] = o2[BLOCK:].astype(BF16)


def _attn_call(sinks, layer, q, k, vt, bias_t):
    b, s, _ = q.shape
    tq = min(ATTN_Q_TILE, s)
    r = tq // BLOCK
    nb = s // BLOCK
    assert s % tq == 0 and s >= 2 * BLOCK and WINDOW <= BLOCK
    prev_i = lambda i: jnp.maximum(i * r - 1, 0)
    next_i = lambda i: jnp.minimum((i + 1) * r, nb - 1)
    cur = pl.BlockSpec((None, tq, D_MODEL), lambda bi, i: (bi, i, 0))
    prev = pl.BlockSpec((None, BLOCK, D_MODEL), lambda bi, i: (bi, prev_i(i), 0))
    nxt = pl.BlockSpec((None, BLOCK, D_MODEL), lambda bi, i: (bi, next_i(i), 0))
    vcur = pl.BlockSpec((None, KV_DIM, tq), lambda bi, i: (bi, 0, i))
    vprev = pl.BlockSpec((None, KV_DIM, BLOCK), lambda bi, i: (bi, 0, prev_i(i)))
    vnxt = pl.BlockSpec((None, KV_DIM, BLOCK), lambda bi, i: (bi, 0, next_i(i)))
    return pl.pallas_call(
        functools.partial(_attn_kernel, layer=layer),
        out_shape=jax.ShapeDtypeStruct((b, s, D_MODEL), BF16),
        grid=(b, s // tq),
        in_specs=[pl.BlockSpec(memory_space=pltpu.SMEM), cur, prev, cur, nxt, vprev, vcur, vnxt,
                  _resident((3, 3 * BLOCK, BLOCK))],
        out_specs=cur,
        compiler_params=_params(2),
        name="band_attention",
    )(sinks, q, k, k, k, vt, vt, vt, bias_t)


def _proj_ffn_rows(x, mix, wp_ref, g_ref, wgu_ref, wd_ref):
    x1 = x + jnp.dot(mix.astype(BF16), wp_ref[...], preferred_element_type=F32)
    xn = (x1 * _rms_scale(x1) * g_ref[...]).astype(BF16)
    acc = x1
    for c in range(D_FF // FFN_CHUNK):
        lo = c * FFN_CHUNK
        gate = jnp.dot(xn, wgu_ref[:, lo:lo + FFN_CHUNK], preferred_element_type=F32)
        up = jnp.dot(xn, wgu_ref[:, D_FF + lo:D_FF + lo + FFN_CHUNK], preferred_element_type=F32)
        hid = (gate * jax.nn.sigmoid(gate) * up).astype(BF16)
        acc = acc + jnp.dot(hid, wd_ref[lo:lo + FFN_CHUNK, :], preferred_element_type=F32)
    return acc


def _ffn_weight_specs(mixer_layer, layer):
    return [_layer((D_MODEL, D_MODEL), mixer_layer), _layer((1, D_MODEL), layer),
            _layer((D_MODEL, 2 * D_FF), layer), _layer((D_FF, D_MODEL), layer)]


def _proj_ffn_kernel(x_ref, mix_ref, wp_ref, g_ref, wgu_ref, wd_ref, o_ref):
    o_ref[...] = _proj_ffn_rows(x_ref[...], mix_ref[...], wp_ref, g_ref, wgu_ref, wd_ref)


def _proj_ffn_call(x, mix, mixer_layer, layer, wp, g, wgu, wd):
    b, s, _ = x.shape
    t = b * s
    tm = min(FFN_ROW_TILE, t)
    assert t % tm == 0
    row = pl.BlockSpec((tm, D_MODEL), lambda i: (i, 0))
    return pl.pallas_call(
        _proj_ffn_kernel,
        out_shape=jax.ShapeDtypeStruct((t, D_MODEL), F32),
        grid=(t // tm,),
        in_specs=[row, row] + _ffn_weight_specs(mixer_layer, layer),
        out_specs=row,
        compiler_params=_params(1),
        name="proj_ffn",
    )(x.reshape(t, D_MODEL), mix.reshape(t, D_MODEL), wp, g, wgu, wd).reshape(b, s, D_MODEL)


def _dft1_kernel(x_ref, g_ref, wc_ref, m1_ref, y_ref):
    n1 = x_ref.shape[0]
    gd = FOURIER_GROUP_DIM
    for h in range(DFT_STEP // DFT_COLS):
        js = range(h * DFT_COLS, (h + 1) * DFT_COLS)
        xs = jnp.concatenate([x_ref[:, j, :] for j in js], axis=0)
        xn = (xs * _rms_scale(xs) * g_ref[...]).astype(BF16)
        z = [jnp.dot(xn[:, gi * gd:(gi + 1) * gd], wc_ref[...], preferred_element_type=F32)
             for gi in range(N_FOURIER_GROUPS)]
        zr = jnp.concatenate([zg[:, :gd] for zg in z], axis=1).astype(BF16)
        zi = jnp.concatenate([zg[:, gd:] for zg in z], axis=1).astype(BF16)
        rhs = jnp.concatenate(
            [jnp.concatenate([zr[jj * n1:(jj + 1) * n1], zi[jj * n1:(jj + 1) * n1]], axis=0)
             for jj in range(DFT_COLS)], axis=1)
        y = jnp.dot(m1_ref[...], rhs, preferred_element_type=F32).astype(BF16)
        yw = pltpu.bitcast(y, jnp.uint32)
        for jj, j in enumerate(js):
            y_ref[j] = yw[:, jj * D_MODEL:(jj + 1) * D_MODEL]


def _dft1_call(x, layer, g, wc, m1):
    b, s, _ = x.shape
    n2 = BLOCK
    n1 = s // n2
    assert s % n2 == 0 and n1 % DFT_STEP == 0 and n2 % DFT_STEP == 0
    return pl.pallas_call(
        _dft1_kernel,
        out_shape=jax.ShapeDtypeStruct((b, n2, n1, D_MODEL), jnp.uint32),
        grid=(b, n2 // DFT_STEP),
        in_specs=[pl.BlockSpec((None, n1, DFT_STEP, D_MODEL), lambda bi, i: (bi, 0, i, 0)),
                  _layer((1, D_MODEL), layer),
                  _resident((FOURIER_GROUP_DIM, 2 * FOURIER_GROUP_DIM)),
                  _resident((2 * n1, 2 * n1))],
        out_specs=pl.BlockSpec((None, DFT_STEP, n1, D_MODEL), lambda bi, i: (bi, i, 0, 0)),
        compiler_params=_params(2),
        name="dft_stage1",
    )(x.reshape(b, n1, n2, D_MODEL), g, wc, m1)


def _dft2_kernel(y_ref, m2_ref, o_ref):
    for j in range(DFT_STEP):
        rhs = pltpu.bitcast(y_ref[:, j, :], BF16)
        o_ref[:, j, :] = jnp.dot(m2_ref[j], rhs, preferred_element_type=F32)


def _dft2_call(y, m2, s):
    b = y.shape[0]
    n2 = BLOCK
    n1 = s // n2
    blk = pl.BlockSpec((None, n2, DFT_STEP, D_MODEL), lambda bi, i: (bi, 0, i, 0))
    return pl.pallas_call(
        _dft2_kernel,
        out_shape=jax.ShapeDtypeStruct((b, n2, n1, D_MODEL), F32),
        grid=(b, n1 // DFT_STEP),
        in_specs=[blk, pl.BlockSpec((DFT_STEP, n2, 2 * n2), lambda bi, i: (i, 0, 0))],
        out_specs=blk,
        compiler_params=_params(2),
        name="dft_stage2",
    )(y, m2).reshape(b, s, D_MODEL)


def _rope_tables(s):
    half = HEAD_DIM // 2
    inv_freq = ROPE_THETA ** (-jnp.arange(half, dtype=F32) / half)
    ang = jnp.arange(s).astype(F32)[:, None] * inv_freq[None, :]
    reps = LANES // half
    sign = np.where(np.arange(reps) % 2 == 0, -1.0, 1.0).astype(np.float32)[None, :, None]
    cos_t = jnp.broadcast_to(jnp.cos(ang)[:, None, :], (s, reps, half)).reshape(s, LANES)
    sin_t = (jnp.sin(ang)[:, None, :] * sign).reshape(s, LANES)
    return cos_t, sin_t


def _band_bias():
    qi = np.arange(BLOCK)[:, None]
    kj = np.arange(3 * BLOCK)[None, :]
    band = np.abs(kj - BLOCK - qi) <= WINDOW
    mid = np.where(band, 0.0, NEG_INF)
    first = np.where(band & (kj >= BLOCK), 0.0, NEG_INF)
    last = np.where(band & (kj < 2 * BLOCK), 0.0, NEG_INF)
    return jnp.asarray(np.stack([mid.T, first.T, last.T]), dtype=F32)


def _segment_ones():
    idx = np.arange(MXU_DIM) // HEAD_DIM
    return jnp.asarray(idx[:, None] == idx[None, :], dtype=BF16)


def _channel_dft():
    c = np.arange(FOURIER_GROUP_DIM)
    ang = 2.0 * np.pi * ((c[:, None] * c[None, :]) % FOURIER_GROUP_DIM) / FOURIER_GROUP_DIM
    return jnp.asarray(np.concatenate([np.cos(ang), -np.sin(ang)], axis=1), dtype=F32).astype(BF16)


def _stage1_dft(n1):
    k = np.arange(n1)
    ang = 2.0 * np.pi * ((k[:, None] * k[None, :]) % n1) / n1
    c, s = np.cos(ang), np.sin(ang)
    m = np.stack([np.concatenate([c, s], axis=1), np.concatenate([-s, c], axis=1)], axis=1)
    return jnp.asarray(m.reshape(2 * n1, 2 * n1), dtype=F32).astype(BF16)


def _stage2_dft(s):
    n2 = BLOCK
    n1 = s // n2
    k1 = np.arange(n1)[:, None]
    k2 = np.arange(n2)[:, None]
    n = np.arange(n2)[None, :]
    a1 = 2.0 * np.pi * ((k1 * n) % s) / s
    a2 = 2.0 * np.pi * ((k2 * n) % n2) / n2
    c1, s1 = jnp.asarray(np.cos(a1), F32)[:, None, :], jnp.asarray(np.sin(a1), F32)[:, None, :]
    c2, s2 = jnp.asarray(np.cos(a2), F32)[None], jnp.asarray(np.sin(a2), F32)[None]
    scale = float(1.0 / np.sqrt(float(s) * FOURIER_GROUP_DIM))
    cos = (c1 * c2 - s1 * s2) * scale
    sin = (s1 * c2 + c1 * s2) * scale
    return jnp.stack([cos, sin], axis=3).reshape(n1, n2, 2 * n2).astype(BF16)


def _trunk(x, p, tables):
    b, s, _ = x.shape
    cos, sin, bias, seg, wc, m1, m2 = tables
    for i in range(DEPTH):
        j = i // 2
        if i % 2 == 0:
            q, k, vt = _qkv_call(x, j, p["attn_norm_g"], p["w_qkv"], p["q_norm_g"],
                                 p["k_norm_g"], cos, sin, seg)
            mix = _attn_call(p["attn_sinks"], j, q, k, vt, bias)
            x = _proj_ffn_call(x, mix, j, i, p["w_o_attn"],
                               p["ffn_norm_g"], p["w_gate_up"], p["w_down"])
        else:
            y = _dft1_call(x, j, p["fourier_norm_g"], wc, m1)
            mix = _dft2_call(y, m2, s)
            x = _proj_ffn_call(x, mix, j, i, p["w_fourier_out"],
                               p["ffn_norm_g"], p["w_gate_up"], p["w_down"])
    return x


def kernel(x_prompt, x_sample, attn_norm_g, w_qkv, q_norm_g, k_norm_g, attn_sinks, w_o_attn,
           fourier_norm_g, w_fourier_out, ffn_norm_g, w_gate_up, w_down):
    p = {
        "attn_norm_g": attn_norm_g[:, None, :],
        "w_qkv": w_qkv.astype(BF16),
        "q_norm_g": jnp.tile(q_norm_g, (1, N_Q_HEADS))[:, None, :],
        "k_norm_g": jnp.tile(k_norm_g, (1, N_KV_HEADS))[:, None, :],
        "attn_sinks": attn_sinks,
        "w_o_attn": w_o_attn.astype(BF16),
        "fourier_norm_g": fourier_norm_g[:, None, :],
        "w_fourier_out": w_fourier_out.astype(BF16),
        "ffn_norm_g": ffn_norm_g[:, None, :],
        "w_gate_up": w_gate_up.astype(BF16),
        "w_down": w_down.astype(BF16),
    }
    bias, seg, wc = _band_bias(), _segment_ones(), _channel_dft()
    cos, sin = _rope_tables(max(x_prompt.shape[1], x_sample.shape[1]))
    outs = []
    for x in (x_prompt, x_sample):
        s = x.shape[1]
        tables = (cos, sin, bias, seg, wc, _stage1_dft(s // BLOCK), _stage2_dft(s))
        outs.append(_trunk(x, p, tables))
    return tuple(outs)
```

```python
import functools

import numpy as np
import jax
import jax.numpy as jnp
from jax import lax
from jax.experimental import pallas as pl
from jax.experimental.pallas import tpu as pltpu

D_MODEL = 1024
HEAD_DIM = 64
N_Q_HEADS = 16
N_KV_HEADS = 4
QKV_DIM = (N_Q_HEADS + 2 * N_KV_HEADS) * HEAD_DIM
KV_DIM = N_KV_HEADS * HEAD_DIM
WINDOW = 128
BLOCK = 128
ROPE_THETA = 10000.0
N_FOURIER_GROUPS = 4
FOURIER_GROUP_DIM = D_MODEL // N_FOURIER_GROUPS
D_FF = 2816
EPS = 1e-6
NEG_INF = -1e30
DEPTH = 4

LANES = 128
SUBLANES = 8
MXU_DIM = 256
VMEM_LIMIT_BYTES = 56 * 1024 * 1024

QKV_ROW_TILE = 1024
FFN_ROW_TILE = 1024
ATTN_Q_TILE = 1024
QKV_SUB_ROWS = 128
FFN_CHUNK = 256
DFT_STEP = 16
DFT_COLS = 4

LOG2_E = 1.4426950408889634
BF16 = jnp.bfloat16
F32 = jnp.float32


def _resident(shape):
    nd = len(shape)
    return pl.BlockSpec(shape, lambda *_: (0,) * nd, pipeline_mode=pl.Buffered(1))


def _layer(shape, layer):
    nd = len(shape)
    return pl.BlockSpec((None,) + tuple(shape), lambda *_: (layer,) + (0,) * nd,
                        pipeline_mode=pl.Buffered(1))


def _params(n_axes):
    return pltpu.CompilerParams(
        dimension_semantics=("arbitrary",) * n_axes,
        vmem_limit_bytes=VMEM_LIMIT_BYTES,
    )


def _rms_scale(x):
    return lax.rsqrt(jnp.mean(x * x, axis=-1, keepdims=True) + EPS)


def _head_mean_square(t, seg_ref):
    sq = (t * t).astype(BF16)
    seg = seg_ref[...]
    cols = []
    for c in range(t.shape[1] // MXU_DIM):
        sl = slice(c * MXU_DIM, (c + 1) * MXU_DIM)
        cols.append(jnp.dot(sq[:, sl], seg, preferred_element_type=F32))
    ss = cols[0] if len(cols) == 1 else jnp.concatenate(cols, axis=1)
    return ss * (1.0 / HEAD_DIM)


def _rope(t, cos, sin_signed):
    half = HEAD_DIM // 2
    lane = lax.broadcasted_iota(jnp.int32, (t.shape[0], LANES), 1)
    first_half = (lane % HEAD_DIM) < half
    outs = []
    for c in range(t.shape[1] // LANES):
        tc = t[:, c * LANES:(c + 1) * LANES]
        fwd = pltpu.roll(tc, LANES - half, axis=1)
        bwd = pltpu.roll(tc, half, axis=1)
        partner = jnp.where(first_half, fwd, bwd)
        outs.append(tc * cos + partner * sin_signed)
    return outs


def _expand_kv(chunks):
    lane = lax.broadcasted_iota(jnp.int32, chunks[0].shape, 1)
    low = lane < HEAD_DIM
    zero = jnp.zeros_like(chunks[0])
    outs = []
    for tc in chunks:
        sw = pltpu.roll(tc, HEAD_DIM, axis=1)
        outs += [jnp.where(low, tc, zero), jnp.where(low, zero, sw),
                 jnp.where(low, sw, zero), jnp.where(low, zero, tc)]
    return jnp.concatenate(outs, axis=1)


def _qkv_kernel(x_ref, g_ref, w_ref, gq_ref, gk_ref, cos_ref, sin_ref, seg_ref,
                q_ref, k_ref, vt_ref):
    scale = HEAD_DIM ** -0.5 * LOG2_E
    for r in range(x_ref.shape[0] // QKV_SUB_ROWS):
        rows = slice(r * QKV_SUB_ROWS, (r + 1) * QKV_SUB_ROWS)
        x = x_ref[rows, :]
        xn = (x * _rms_scale(x) * g_ref[...]).astype(BF16)
        qkv = jnp.dot(xn, w_ref[...], preferred_element_type=F32)
        q = qkv[:, :D_MODEL]
        k = qkv[:, D_MODEL:D_MODEL + KV_DIM]
        v = qkv[:, D_MODEL + KV_DIM:]
        cos = cos_ref[rows, :]
        sin = sin_ref[rows, :]
        qn = q * lax.rsqrt(_head_mean_square(q, seg_ref) + EPS) * gq_ref[...]
        kn = k * lax.rsqrt(_head_mean_square(k, seg_ref) + EPS) * gk_ref[...]
        q_ref[rows, :] = (jnp.concatenate(_rope(qn, cos, sin), axis=1) * scale).astype(BF16)
        k_ref[rows, :] = _expand_kv(_rope(kn, cos, sin)).astype(BF16)
        vt_ref[:, rows] = v.T.astype(BF16)


def _qkv_call(x, layer, g, w, gq, gk, cos, sin, seg):
    b, s, _ = x.shape
    tm = min(QKV_ROW_TILE, s)
    assert s % tm == 0 and tm % QKV_SUB_ROWS == 0
    row = lambda width: pl.BlockSpec((None, tm, width), lambda bi, i: (bi, i, 0))
    tab = pl.BlockSpec((tm, LANES), lambda bi, i: (i, 0))
    out = jax.ShapeDtypeStruct((b, s, D_MODEL), BF16)
    return pl.pallas_call(
        _qkv_kernel,
        out_shape=(out, out, jax.ShapeDtypeStruct((b, KV_DIM, s), BF16)),
        grid=(b, s // tm),
        in_specs=[row(D_MODEL), _layer((1, D_MODEL), layer), _layer((D_MODEL, QKV_DIM), layer),
                  _layer((1, D_MODEL), layer), _layer((1, KV_DIM), layer), tab, tab,
                  _resident((MXU_DIM, MXU_DIM))],
        out_specs=(row(D_MODEL), row(D_MODEL),
                   pl.BlockSpec((None, KV_DIM, tm), lambda bi, i: (bi, 0, i))),
        compiler_params=_params(2),
        name="qkv_rope",
    )(x, g, w, gq, gk, cos, sin, seg)


def _attn_kernel(sink_ref, q_ref, kp_ref, kc_ref, kn_ref, vp_ref, vc_ref, vn_ref, bias_ref,
                 o_ref, *, layer):
    tq = q_ref.shape[0]
    i = pl.program_id(1)
    last = pl.num_programs(1) - 1
    n_sub = tq // BLOCK
    n_stage = n_sub * N_KV_HEADS
    keys = 3 * BLOCK
    row = lax.broadcasted_iota(jnp.int32, (LANES, 2 * BLOCK), 0)
    top = row < HEAD_DIM
    vzero = jnp.zeros((HEAD_DIM, keys), BF16)
    ones_row = lax.broadcasted_iota(jnp.int32, (2 * SUBLANES, 2 * keys), 0)
    ones_col = lax.broadcasted_iota(jnp.int32, (2 * SUBLANES, 2 * keys), 1)
    ones_rows = jnp.where((ones_row == 0) & (ones_col < keys) | (ones_row == 1) & (ones_col >= keys),
                          1.0, 0.0).astype(BF16)

    def key_rows(j, lanes):
        parts = []
        if j == 0:
            parts.append(kp_ref[:, lanes])
        parts.append(kc_ref[max(j - 1, 0) * BLOCK:min(j + 2, n_sub) * BLOCK, lanes])
        if j == n_sub - 1:
            parts.append(kn_ref[:, lanes])
        return jnp.concatenate(parts, axis=0)

    def value_cols(j, rows):
        parts = []
        if j == 0:
            parts.append(vp_ref[rows, :])
        parts.append(vc_ref[rows, max(j - 1, 0) * BLOCK:min(j + 2, n_sub) * BLOCK])
        if j == n_sub - 1:
            parts.append(vn_ref[rows, :])
        return jnp.concatenate(parts, axis=1)

    def bias_for(j):
        b0 = bias_ref[0]
        if j == 0:
            b0 = jnp.where(i == 0, bias_ref[1], b0)
        if j == n_sub - 1:
            b0 = jnp.where(i == last, bias_ref[2], b0)
        return b0

    def scores(t):
        j, h = divmod(t, N_KV_HEADS)
        c0 = h * 2 * LANES
        qrows = slice(j * BLOCK, (j + 1) * BLOCK)
        q2 = jnp.concatenate([q_ref[qrows, c0:c0 + LANES],
                              q_ref[qrows, c0 + LANES:c0 + 2 * LANES]], axis=0)
        kab = jnp.concatenate([key_rows(j, slice(c0, c0 + LANES)),
                               key_rows(j, slice(c0 + LANES, c0 + 2 * LANES))], axis=0)
        return lax.dot_general(kab, q2, (((1,), (1,)), ((), ())), preferred_element_type=F32)

    st_next = scores(0)
    bias = None
    for t in range(n_stage):
        j, h = divmod(t, N_KV_HEADS)
        if h == 0:
            bias = bias_for(j)
        c0 = h * 2 * LANES
        qrows = slice(j * BLOCK, (j + 1) * BLOCK)
        st = st_next
        if t + 1 < n_stage:
            st_next = scores(t + 1)
        vt = value_cols(j, slice(h * HEAD_DIM, (h + 1) * HEAD_DIM))
        vabt = jnp.concatenate([jnp.concatenate([vt, vzero], axis=1),
                                jnp.concatenate([vzero, vt], axis=1),
                                ones_rows], axis=0)
        p_rows = []
        e_rows = []
        for ab in range(2):
            p_cols = []
            e_cols = []
            for ch in range(2):
                sink = sink_ref[layer, 4 * h + 2 * ch + ab] * LOG2_E
                sc = st[ab * keys:(ab + 1) * keys, ch * BLOCK:(ch + 1) * BLOCK]
                sc = jnp.concatenate([sc[:BLOCK] + bias[:BLOCK], sc[BLOCK:2 * BLOCK],
                                      sc[2 * BLOCK:] + bias[2 * BLOCK:]], axis=0)
                m = jnp.maximum(jnp.max(sc, axis=0, keepdims=True), sink)
                e_cols.append(jnp.exp2(sink - m))
                p_cols.append(jnp.exp2(sc - m).astype(BF16))
            p_rows.append(jnp.concatenate(p_cols, axis=1))
            e_rows.append(jnp.concatenate(e_cols, axis=1))
        pt = jnp.concatenate(p_rows, axis=0)
        o2t = jnp.dot(vabt, pt, preferred_element_type=F32)
        inv_a = 1.0 / (o2t[LANES:LANES + 1] + e_rows[0])
        inv_b = 1.0 / (o2t[LANES + 1:LANES + 2] + e_rows[1])
        o2 = (o2t[:LANES] * jnp.where(top, inv_a, inv_b)).T
        o_ref[qrows, c0:c0 + LANES] = o2[:BLOCK].astype(BF16)
        o_ref[qrows, c0 + LANES:c0 + 2 * LANES] = o2[BLOCK:].astype(BF16)


def _attn_call(sinks, layer, q, k, vt, bias_t):
    b, s, _ = q.shape
    tq = min(ATTN_Q_TILE, s)
    r = tq // BLOCK
    nb = s // BLOCK
    assert s % tq == 0 and s >= 2 * BLOCK and WINDOW <= BLOCK
    prev_i = lambda i: jnp.maximum(i * r - 1, 0)
    next_i = lambda i: jnp.minimum((i + 1) * r, nb - 1)
    cur = pl.BlockSpec((None, tq, D_MODEL), lambda bi, i: (bi, i, 0))
    prev = pl.BlockSpec((None, BLOCK, D_MODEL), lambda bi, i: (bi, prev_i(i), 0))
    nxt = pl.BlockSpec((None, BLOCK, D_MODEL), lambda bi, i: (bi, next_i(i), 0))
    vcur = pl.BlockSpec((None, KV_DIM, tq), lambda bi, i: (bi, 0, i))
    vprev = pl.BlockSpec((None, KV_DIM, BLOCK), lambda bi, i: (bi, 0, prev_i(i)))
    vnxt = pl.BlockSpec((None, KV_DIM, BLOCK), lambda bi, i: (bi, 0, next_i(i)))
    return pl.pallas_call(
        functools.partial(_attn_kernel, layer=layer),
        out_shape=jax.ShapeDtypeStruct((b, s, D_MODEL), BF16),
        grid=(b, s // tq),
        in_specs=[pl.BlockSpec(memory_space=pltpu.SMEM), cur, prev, cur, nxt, vprev, vcur, vnxt,
                  _resident((3, 3 * BLOCK, BLOCK))],
        out_specs=cur,
        compiler_params=_params(2),
        name="band_attention",
    )(sinks, q, k, k, k, vt, vt, vt, bias_t)


def _proj_ffn_rows(x, mix, wp_ref, g_ref, wgu_ref, wd_ref):
    x1 = x + jnp.dot(mix.astype(BF16), wp_ref[...], preferred_element_type=F32)
    xn = (x1 * _rms_scale(x1) * g_ref[...]).astype(BF16)
    acc = x1
    for c in range(D_FF // FFN_CHUNK):
        lo = c * FFN_CHUNK
        gate = jnp.dot(xn, wgu_ref[:, lo:lo + FFN_CHUNK], preferred_element_type=F32)
        up = jnp.dot(xn, wgu_ref[:, D_FF + lo:D_FF + lo + FFN_CHUNK], preferred_element_type=F32)
        hid = (gate * jax.nn.sigmoid(gate) * up).astype(BF16)
        acc = acc + jnp.dot(hid, wd_ref[lo:lo + FFN_CHUNK, :], preferred_element_type=F32)
    return acc


def _ffn_weight_specs(mixer_layer, layer):
    return [_layer((D_MODEL, D_MODEL), mixer_layer), _layer((1, D_MODEL), layer),
            _layer((D_MODEL, 2 * D_FF), layer), _layer((D_FF, D_MODEL), layer)]


def _proj_ffn_kernel(x_ref, mix_ref, wp_ref, g_ref, wgu_ref, wd_ref, o_ref):
    o_ref[...] = _proj_ffn_rows(x_ref[...], mix_ref[...], wp_ref, g_ref, wgu_ref, wd_ref)


def _proj_ffn_call(x, mix, mixer_layer, layer, wp, g, wgu, wd):
    b, s, _ = x.shape
    t = b * s
    tm = min(FFN_ROW_TILE, t)
    assert t % tm == 0
    row = pl.BlockSpec((tm, D_MODEL), lambda i: (i, 0))
    return pl.pallas_call(
        _proj_ffn_kernel,
        out_shape=jax.ShapeDtypeStruct((t, D_MODEL), F32),
        grid=(t // tm,),
        in_specs=[row, row] + _ffn_weight_specs(mixer_layer, layer),
        out_specs=row,
        compiler_params=_params(1),
        name="proj_ffn",
    )(x.reshape(t, D_MODEL), mix.reshape(t, D_MODEL), wp, g, wgu, wd).reshape(b, s, D_MODEL)


def _dft1_kernel(x_ref, g_ref, wc_ref, m1_ref, y_ref):
    n1 = x_ref.shape[0]
    gd = FOURIER_GROUP_DIM
    for h in range(DFT_STEP // DFT_COLS):
        js = range(h * DFT_COLS, (h + 1) * DFT_COLS)
        xs = jnp.concatenate([x_ref[:, j, :] for j in js], axis=0)
        xn = (xs * _rms_scale(xs) * g_ref[...]).astype(BF16)
        z = [jnp.dot(xn[:, gi * gd:(gi + 1) * gd], wc_ref[...], preferred_element_type=F32)
             for gi in range(N_FOURIER_GROUPS)]
        zr = jnp.concatenate([zg[:, :gd] for zg in z], axis=1).astype(BF16)
        zi = jnp.concatenate([zg[:, gd:] for zg in z], axis=1).astype(BF16)
        rhs = jnp.concatenate(
            [jnp.concatenate([zr[jj * n1:(jj + 1) * n1], zi[jj * n1:(jj + 1) * n1]], axis=0)
             for jj in range(DFT_COLS)], axis=1)
        y = jnp.dot(m1_ref[...], rhs, preferred_element_type=F32).astype(BF16)
        yw = pltpu.bitcast(y, jnp.uint32)
        for jj, j in enumerate(js):
            y_ref[j] = yw[:, jj * D_MODEL:(jj + 1) * D_MODEL]


def _dft1_call(x, layer, g, wc, m1):
    b, s, _ = x.shape
    n2 = BLOCK
    n1 = s // n2
    assert s % n2 == 0 and n1 % DFT_STEP == 0 and n2 % DFT_STEP == 0
    return pl.pallas_call(
        _dft1_kernel,
        out_shape=jax.ShapeDtypeStruct((b, n2, n1, D_MODEL), jnp.uint32),
        grid=(b, n2 // DFT_STEP),
        in_specs=[pl.BlockSpec((None, n1, DFT_STEP, D_MODEL), lambda bi, i: (bi, 0, i, 0)),
                  _layer((1, D_MODEL), layer),
                  _resident((FOURIER_GROUP_DIM, 2 * FOURIER_GROUP_DIM)),
                  _resident((2 * n1, 2 * n1))],
        out_specs=pl.BlockSpec((None, DFT_STEP, n1, D_MODEL), lambda bi, i: (bi, i, 0, 0)),
        compiler_params=_params(2),
        name="dft_stage1",
    )(x.reshape(b, n1, n2, D_MODEL), g, wc, m1)


def _dft2_kernel(y_ref, m2_ref, o_ref):
    for j in range(DFT_STEP):
        rhs = pltpu.bitcast(y_ref[:, j, :], BF16)
        o_ref[:, j, :] = jnp.dot(m2_ref[j], rhs, preferred_element_type=F32)


def _dft2_call(y, m2, s):
    b = y.shape[0]
    n2 = BLOCK
    n1 = s // n2
    blk = pl.BlockSpec((None, n2, DFT_STEP, D_MODEL), lambda bi, i: (bi, 0, i, 0))
    return pl.pallas_call(
        _dft2_kernel,
        out_shape=jax.ShapeDtypeStruct((b, n2, n1, D_MODEL), F32),
        grid=(b, n1 // DFT_STEP),
        in_specs=[blk, pl.BlockSpec((DFT_STEP, n2, 2 * n2), lambda bi, i: (i, 0, 0))],
        out_specs=blk,
        compiler_params=_params(2),
        name="dft_stage2",
    )(y, m2).reshape(b, s, D_MODEL)


def _rope_tables(s):
    half = HEAD_DIM // 2
    inv_freq = ROPE_THETA ** (-jnp.arange(half, dtype=F32) / half)
    ang = jnp.arange(s).astype(F32)[:, None] * inv_freq[None, :]
    cos = jnp.cos(ang)
    sin = jnp.sin(ang)
    reps = LANES // HEAD_DIM
    cos_t = jnp.tile(jnp.concatenate([cos, cos], axis=1), (1, reps))
    sin_t = jnp.tile(jnp.concatenate([-sin, sin], axis=1), (1, reps))
    return cos_t, sin_t


def _band_bias():
    qi = np.arange(BLOCK)[:, None]
    kj = np.arange(3 * BLOCK)[None, :]
    band = np.abs(kj - BLOCK - qi) <= WINDOW
    mid = np.where(band, 0.0, NEG_INF)
    first = np.where(band & (kj >= BLOCK), 0.0, NEG_INF)
    last = np.where(band & (kj < 2 * BLOCK), 0.0, NEG_INF)
    return jnp.asarray(np.stack([mid.T, first.T, last.T]), dtype=F32)


def _segment_ones():
    idx = np.arange(MXU_DIM) // HEAD_DIM
    return jnp.asarray(idx[:, None] == idx[None, :], dtype=BF16)


def _channel_dft():
    c = np.arange(FOURIER_GROUP_DIM)
    ang = 2.0 * np.pi * ((c[:, None] * c[None, :]) % FOURIER_GROUP_DIM) / FOURIER_GROUP_DIM
    return jnp.asarray(np.concatenate([np.cos(ang), -np.sin(ang)], axis=1), dtype=F32).astype(BF16)


def _stage1_dft(n1):
    k = np.arange(n1)
    ang = 2.0 * np.pi * ((k[:, None] * k[None, :]) % n1) / n1
    c, s = np.cos(ang), np.sin(ang)
    m = np.stack([np.concatenate([c, s], axis=1), np.concatenate([-s, c], axis=1)], axis=1)
    return jnp.asarray(m.reshape(2 * n1, 2 * n1), dtype=F32).astype(BF16)


def _stage2_dft(s):
    n2 = BLOCK
    n1 = s // n2
    k1 = np.arange(n1)[:, None]
    k2 = np.arange(n2)[:, None]
    n = np.repeat(np.arange(n2), 2)[None, :]
    quarter = np.tile([0.0, 0.5 * np.pi], n2)[None, :]
    a1 = 2.0 * np.pi * ((k1 * n) % s) / s
    a2 = 2.0 * np.pi * ((k2 * n) % n2) / n2 - quarter
    c1, s1 = jnp.asarray(np.cos(a1), F32)[:, None, :], jnp.asarray(np.sin(a1), F32)[:, None, :]
    c2, s2 = jnp.asarray(np.cos(a2), F32)[None], jnp.asarray(np.sin(a2), F32)[None]
    scale = float(1.0 / np.sqrt(float(s) * FOURIER_GROUP_DIM))
    return ((c1 * c2 - s1 * s2) * scale).astype(BF16)


def _trunk(x, p, tables):
    b, s, _ = x.shape
    cos, sin, bias, seg, wc, m1, m2 = tables
    for i in range(DEPTH):
        j = i // 2
        if i % 2 == 0:
            q, k, vt = _qkv_call(x, j, p["attn_norm_g"], p["w_qkv"], p["q_norm_g"],
                                 p["k_norm_g"], cos, sin, seg)
            mix = _attn_call(p["attn_sinks"], j, q, k, vt, bias)
            x = _proj_ffn_call(x, mix, j, i, p["w_o_attn"],
                               p["ffn_norm_g"], p["w_gate_up"], p["w_down"])
        else:
            y = _dft1_call(x, j, p["fourier_norm_g"], wc, m1)
            mix = _dft2_call(y, m2, s)
            x = _proj_ffn_call(x, mix, j, i, p["w_fourier_out"],
                               p["ffn_norm_g"], p["w_gate_up"], p["w_down"])
    return x


def kernel(x_prompt, x_sample, attn_norm_g, w_qkv, q_norm_g, k_norm_g, attn_sinks, w_o_attn,
           fourier_norm_g, w_fourier_out, ffn_norm_g, w_gate_up, w_down):
    p = {
        "attn_norm_g": attn_norm_g[:, None, :],
        "w_qkv": w_qkv.astype(BF16),
        "q_norm_g": jnp.tile(q_norm_g, (1, N_Q_HEADS))[:, None, :],
        "k_norm_g": jnp.tile(k_norm_g, (1, N_KV_HEADS))[:, None, :],
        "attn_sinks": attn_sinks,
        "w_o_attn": w_o_attn.astype(BF16),
        "fourier_norm_g": fourier_norm_g[:, None, :],
        "w_fourier_out": w_fourier_out.astype(BF16),
        "ffn_norm_g": ffn_norm_g[:, None, :],
        "w_gate_up": w_gate_up.astype(BF16),
        "w_down": w_down.astype(BF16),
    }
    bias, seg, wc = _band_bias(), _segment_ones(), _channel_dft()
    cos, sin = _rope_tables(max(x_prompt.shape[1], x_sample.shape[1]))
    outs = []
    for x in (x_prompt, x_sample):
        s = x.shape[1]
        tables = (cos, sin, bias, seg, wc, _stage1_dft(s // BLOCK), _stage2_dft(s))
        outs.append(_trunk(x, p, tables))
    return tuple(outs)
```

```python
import functools

import numpy as np
import jax
import jax.numpy as jnp
from jax import lax
from jax.experimental import pallas as pl
from jax.experimental.pallas import tpu as pltpu

D_MODEL = 1024
HEAD_DIM = 64
N_Q_HEADS = 16
N_KV_HEADS = 4
QKV_DIM = (N_Q_HEADS + 2 * N_KV_HEADS) * HEAD_DIM
KV_DIM = N_KV_HEADS * HEAD_DIM
WINDOW = 128
BLOCK = 128
ROPE_THETA = 10000.0
N_FOURIER_GROUPS = 4
FOURIER_GROUP_DIM = D_MODEL // N_FOURIER_GROUPS
D_FF = 2816
EPS = 1e-6
NEG_INF = -1e30
DEPTH = 4

LANES = 128
SUBLANES = 8
MXU_DIM = 256
VMEM_LIMIT_BYTES = 56 * 1024 * 1024

QKV_ROW_TILE = 1024
FFN_ROW_TILE = 1024
ATTN_Q_TILE = 1024
QKV_SUB_ROWS = 128
FFN_CHUNK = 256
DFT_STEP = 16
DFT_COLS = 4

LOG2_E = 1.4426950408889634
BF16 = jnp.bfloat16
F32 = jnp.float32


def _resident(shape):
    nd = len(shape)
    return pl.BlockSpec(shape, lambda *_: (0,) * nd, pipeline_mode=pl.Buffered(1))


def _layer(shape, layer):
    nd = len(shape)
    return pl.BlockSpec((None,) + tuple(shape), lambda *_: (layer,) + (0,) * nd,
                        pipeline_mode=pl.Buffered(1))


def _params(n_axes):
    return pltpu.CompilerParams(
        dimension_semantics=("arbitrary",) * n_axes,
        vmem_limit_bytes=VMEM_LIMIT_BYTES,
    )


def _rms_scale(x):
    return lax.rsqrt(jnp.mean(x * x, axis=-1, keepdims=True) + EPS)


def _head_mean_square(t, seg_ref):
    sq = (t * t).astype(BF16)
    seg = seg_ref[...]
    cols = []
    for c in range(t.shape[1] // MXU_DIM):
        sl = slice(c * MXU_DIM, (c + 1) * MXU_DIM)
        cols.append(jnp.dot(sq[:, sl], seg, preferred_element_type=F32))
    ss = cols[0] if len(cols) == 1 else jnp.concatenate(cols, axis=1)
    return ss * (1.0 / HEAD_DIM)


def _rope(t, cos, sin_signed):
    half = HEAD_DIM // 2
    lane = lax.broadcasted_iota(jnp.int32, (t.shape[0], LANES), 1)
    first_half = (lane % HEAD_DIM) < half
    outs = []
    for c in range(t.shape[1] // LANES):
        tc = t[:, c * LANES:(c + 1) * LANES]
        fwd = pltpu.roll(tc, LANES - half, axis=1)
        bwd = pltpu.roll(tc, half, axis=1)
        partner = jnp.where(first_half, fwd, bwd)
        outs.append(tc * cos + partner * sin_signed)
    return outs


def _expand_kv(chunks):
    lane = lax.broadcasted_iota(jnp.int32, chunks[0].shape, 1)
    low = lane < HEAD_DIM
    zero = jnp.zeros_like(chunks[0])
    outs = []
    for tc in chunks:
        sw = pltpu.roll(tc, HEAD_DIM, axis=1)
        outs += [jnp.where(low, tc, zero), jnp.where(low, zero, sw),
                 jnp.where(low, sw, zero), jnp.where(low, zero, tc)]
    return jnp.concatenate(outs, axis=1)


def _qkv_kernel(x_ref, g_ref, w_ref, gq_ref, gk_ref, cos_ref, sin_ref, seg_ref,
                q_ref, k_ref, vt_ref):
    scale = HEAD_DIM ** -0.5 * LOG2_E
    for r in range(x_ref.shape[0] // QKV_SUB_ROWS):
        rows = slice(r * QKV_SUB_ROWS, (r + 1) * QKV_SUB_ROWS)
        x = x_ref[rows, :]
        xn = (x * _rms_scale(x) * g_ref[...]).astype(BF16)
        qkv = jnp.dot(xn, w_ref[...], preferred_element_type=F32)
        q = qkv[:, :D_MODEL]
        k = qkv[:, D_MODEL:D_MODEL + KV_DIM]
        v = qkv[:, D_MODEL + KV_DIM:]
        cos = jnp.concatenate([cos_ref[rows, :]] * (LANES // HEAD_DIM), axis=1)
        sin = jnp.concatenate([sin_ref[rows, :]] * (LANES // HEAD_DIM), axis=1)
        qn = q * lax.rsqrt(_head_mean_square(q, seg_ref) + EPS) * gq_ref[...]
        kn = k * lax.rsqrt(_head_mean_square(k, seg_ref) + EPS) * gk_ref[...]
        q_ref[rows, :] = (jnp.concatenate(_rope(qn, cos, sin), axis=1) * scale).astype(BF16)
        k_ref[rows, :] = _expand_kv(_rope(kn, cos, sin)).astype(BF16)
        vt_ref[:, rows] = v.T.astype(BF16)


def _qkv_call(x, layer, g, w, gq, gk, cos, sin, seg):
    b, s, _ = x.shape
    tm = min(QKV_ROW_TILE, s)
    assert s % tm == 0 and tm % QKV_SUB_ROWS == 0
    row = lambda width: pl.BlockSpec((None, tm, width), lambda bi, i: (bi, i, 0))
    tab = pl.BlockSpec((tm, HEAD_DIM), lambda bi, i: (i, 0))
    out = jax.ShapeDtypeStruct((b, s, D_MODEL), BF16)
    return pl.pallas_call(
        _qkv_kernel,
        out_shape=(out, out, jax.ShapeDtypeStruct((b, KV_DIM, s), BF16)),
        grid=(b, s // tm),
        in_specs=[row(D_MODEL), _layer((1, D_MODEL), layer), _layer((D_MODEL, QKV_DIM), layer),
                  _layer((1, D_MODEL), layer), _layer((1, KV_DIM), layer), tab, tab,
                  _resident((MXU_DIM, MXU_DIM))],
        out_specs=(row(D_MODEL), row(D_MODEL),
                   pl.BlockSpec((None, KV_DIM, tm), lambda bi, i: (bi, 0, i))),
        compiler_params=_params(2),
        name="qkv_rope",
    )(x, g, w, gq, gk, cos, sin, seg)


def _attn_kernel(sink_ref, q_ref, kp_ref, kc_ref, kn_ref, vp_ref, vc_ref, vn_ref, bias_ref,
                 o_ref, *, layer):
    tq = q_ref.shape[0]
    i = pl.program_id(1)
    last = pl.num_programs(1) - 1
    n_sub = tq // BLOCK
    n_stage = n_sub * N_KV_HEADS
    keys = 3 * BLOCK
    row = lax.broadcasted_iota(jnp.int32, (LANES, 2 * BLOCK), 0)
    top = row < HEAD_DIM
    vzero = jnp.zeros((HEAD_DIM, keys), BF16)
    ones_row = lax.broadcasted_iota(jnp.int32, (2 * SUBLANES, 2 * keys), 0)
    ones_col = lax.broadcasted_iota(jnp.int32, (2 * SUBLANES, 2 * keys), 1)
    ones_rows = jnp.where((ones_row == 0) & (ones_col < keys) | (ones_row == 1) & (ones_col >= keys),
                          1.0, 0.0).astype(BF16)

    def key_rows(j, lanes):
        parts = []
        if j == 0:
            parts.append(kp_ref[:, lanes])
        parts.append(kc_ref[max(j - 1, 0) * BLOCK:min(j + 2, n_sub) * BLOCK, lanes])
        if j == n_sub - 1:
            parts.append(kn_ref[:, lanes])
        return jnp.concatenate(parts, axis=0)

    def value_cols(j, rows):
        parts = []
        if j == 0:
            parts.append(vp_ref[rows, :])
        parts.append(vc_ref[rows, max(j - 1, 0) * BLOCK:min(j + 2, n_sub) * BLOCK])
        if j == n_sub - 1:
            parts.append(vn_ref[rows, :])
        return jnp.concatenate(parts, axis=1)

    def bias_for(j):
        b0 = bias_ref[0]
        if j == 0:
            b0 = jnp.where(i == 0, bias_ref[1], b0)
        if j == n_sub - 1:
            b0 = jnp.where(i == last, bias_ref[2], b0)
        return b0

    def scores(t):
        j, h = divmod(t, N_KV_HEADS)
        c0 = h * 2 * LANES
        qrows = slice(j * BLOCK, (j + 1) * BLOCK)
        q2 = jnp.concatenate([q_ref[qrows, c0:c0 + LANES],
                              q_ref[qrows, c0 + LANES:c0 + 2 * LANES]], axis=0)
        kab = jnp.concatenate([key_rows(j, slice(c0, c0 + LANES)),
                               key_rows(j, slice(c0 + LANES, c0 + 2 * LANES))], axis=0)
        return lax.dot_general(kab, q2, (((1,), (1,)), ((), ())), preferred_element_type=F32)

    st_next = scores(0)
    bias = None
    for t in range(n_stage):
        j, h = divmod(t, N_KV_HEADS)
        if h == 0:
            bias = bias_for(j)
        c0 = h * 2 * LANES
        qrows = slice(j * BLOCK, (j + 1) * BLOCK)
        st = st_next
        if t + 1 < n_stage:
            st_next = scores(t + 1)
        vt = value_cols(j, slice(h * HEAD_DIM, (h + 1) * HEAD_DIM))
        vabt = jnp.concatenate([jnp.concatenate([vt, vzero], axis=1),
                                jnp.concatenate([vzero, vt], axis=1),
                                ones_rows], axis=0)
        p_rows = []
        e_rows = []
        for ab in range(2):
            p_cols = []
            e_cols = []
            for ch in range(2):
                sink = sink_ref[layer, 4 * h + 2 * ch + ab] * LOG2_E
                sc = st[ab * keys:(ab + 1) * keys, ch * BLOCK:(ch + 1) * BLOCK]
                sc = jnp.concatenate([sc[:BLOCK] + bias[:BLOCK], sc[BLOCK:2 * BLOCK],
                                      sc[2 * BLOCK:] + bias[2 * BLOCK:]], axis=0)
                m = jnp.maximum(jnp.max(sc, axis=0, keepdims=True), sink)
                e_cols.append(jnp.exp2(sink - m))
                p_cols.append(jnp.exp2(sc - m).astype(BF16))
            p_rows.append(jnp.concatenate(p_cols, axis=1))
            e_rows.append(jnp.concatenate(e_cols, axis=1))
        pt = jnp.concatenate(p_rows, axis=0)
        o2t = jnp.dot(vabt, pt, preferred_element_type=F32)
        inv_a = 1.0 / (o2t[LANES:LANES + 1] + e_rows[0])
        inv_b = 1.0 / (o2t[LANES + 1:LANES + 2] + e_rows[1])
        o2 = (o2t[:LANES] * jnp.where(top, inv_a, inv_b)).T
        o_ref[qrows, c0:c0 + LANES] = o2[:BLOCK].astype(BF16)
        o_ref[qrows, c0 + LANES:c0 + 2 * LANES] = o2[BLOCK:].astype(BF16)


def _attn_call(sinks, layer, q, k, vt, bias_t):
    b, s, _ = q.shape
    tq = min(ATTN_Q_TILE, s)
    r = tq // BLOCK
    nb = s // BLOCK
    assert s % tq == 0 and s >= 2 * BLOCK and WINDOW <= BLOCK
    prev_i = lambda i: jnp.maximum(i * r - 1, 0)
    next_i = lambda i: jnp.minimum((i + 1) * r, nb - 1)
    cur = pl.BlockSpec((None, tq, D_MODEL), lambda bi, i: (bi, i, 0))
    prev = pl.BlockSpec((None, BLOCK, D_MODEL), lambda bi, i: (bi, prev_i(i), 0))
    nxt = pl.BlockSpec((None, BLOCK, D_MODEL), lambda bi, i: (bi, next_i(i), 0))
    vcur = pl.BlockSpec((None, KV_DIM, tq), lambda bi, i: (bi, 0, i))
    vprev = pl.BlockSpec((None, KV_DIM, BLOCK), lambda bi, i: (bi, 0, prev_i(i)))
    vnxt = pl.BlockSpec((None, KV_DIM, BLOCK), lambda bi, i: (bi, 0, next_i(i)))
    return pl.pallas_call(
        functools.partial(_attn_kernel, layer=layer),
        out_shape=jax.ShapeDtypeStruct((b, s, D_MODEL), BF16),
        grid=(b, s // tq),
        in_specs=[pl.BlockSpec(memory_space=pltpu.SMEM), cur, prev, cur, nxt, vprev, vcur, vnxt,
                  _resident((3, 3 * BLOCK, BLOCK))],
        out_specs=cur,
        compiler_params=_params(2),
        name="band_attention",
    )(sinks, q, k, k, k, vt, vt, vt, bias_t)


def _proj_ffn_rows(x, mix, wp_ref, g_ref, wgu_ref, wd_ref):
    x1 = x + jnp.dot(mix.astype(BF16), wp_ref[...], preferred_element_type=F32)
    xn = (x1 * _rms_scale(x1) * g_ref[...]).astype(BF16)
    acc = x1
    for c in range(D_FF // FFN_CHUNK):
        lo = c * FFN_CHUNK
        gate = jnp.dot(xn, wgu_ref[:, lo:lo + FFN_CHUNK], preferred_element_type=F32)
        up = jnp.dot(xn, wgu_ref[:, D_FF + lo:D_FF + lo + FFN_CHUNK], preferred_element_type=F32)
        hid = (gate * jax.nn.sigmoid(gate) * up).astype(BF16)
        acc = acc + jnp.dot(hid, wd_ref[lo:lo + FFN_CHUNK, :], preferred_element_type=F32)
    return acc


def _ffn_weight_specs(mixer_layer, layer):
    return [_layer((D_MODEL, D_MODEL), mixer_layer), _layer((1, D_MODEL), layer),
            _layer((D_MODEL, 2 * D_FF), layer), _layer((D_FF, D_MODEL), layer)]


def _proj_ffn_kernel(x_ref, mix_ref, wp_ref, g_ref, wgu_ref, wd_ref, o_ref):
    o_ref[...] = _proj_ffn_rows(x_ref[...], mix_ref[...], wp_ref, g_ref, wgu_ref, wd_ref)


def _proj_ffn_call(x, mix, mixer_layer, layer, wp, g, wgu, wd):
    b, s, _ = x.shape
    t = b * s
    tm = min(FFN_ROW_TILE, t)
    assert t % tm == 0
    row = pl.BlockSpec((tm, D_MODEL), lambda i: (i, 0))
    return pl.pallas_call(
        _proj_ffn_kernel,
        out_shape=jax.ShapeDtypeStruct((t, D_MODEL), F32),
        grid=(t // tm,),
        in_specs=[row, row] + _ffn_weight_specs(mixer_layer, layer),
        out_specs=row,
        compiler_params=_params(1),
        name="proj_ffn",
    )(x.reshape(t, D_MODEL), mix.reshape(t, D_MODEL), wp, g, wgu, wd).reshape(b, s, D_MODEL)


def _dft1_kernel(x_ref, g_ref, wc_ref, m1_ref, y_ref):
    n1 = x_ref.shape[0]
    gd = FOURIER_GROUP_DIM
    for h in range(DFT_STEP // DFT_COLS):
        js = range(h * DFT_COLS, (h + 1) * DFT_COLS)
        xs = jnp.concatenate([x_ref[:, j, :] for j in js], axis=0)
        xn = (xs * _rms_scale(xs) * g_ref[...]).astype(BF16)
        z = [jnp.dot(xn[:, gi * gd:(gi + 1) * gd], wc_ref[...], preferred_element_type=F32)
             for gi in range(N_FOURIER_GROUPS)]
        zr = jnp.concatenate([zg[:, :gd] for zg in z], axis=1).astype(BF16)
        zi = jnp.concatenate([zg[:, gd:] for zg in z], axis=1).astype(BF16)
        rhs = jnp.concatenate(
            [jnp.concatenate([zr[jj * n1:(jj + 1) * n1], zi[jj * n1:(jj + 1) * n1]], axis=0)
             for jj in range(DFT_COLS)], axis=1)
        y = jnp.dot(m1_ref[...], rhs, preferred_element_type=F32).astype(BF16)
        yw = pltpu.bitcast(y, jnp.uint32)
        for jj, j in enumerate(js):
            y_ref[j] = yw[:, jj * D_MODEL:(jj + 1) * D_MODEL]


def _dft1_call(x, layer, g, wc, m1):
    b, s, _ = x.shape
    n2 = BLOCK
    n1 = s // n2
    assert s % n2 == 0 and n1 % DFT_STEP == 0 and n2 % DFT_STEP == 0
    return pl.pallas_call(
        _dft1_kernel,
        out_shape=jax.ShapeDtypeStruct((b, n2, n1, D_MODEL), jnp.uint32),
        grid=(b, n2 // DFT_STEP),
        in_specs=[pl.BlockSpec((None, n1, DFT_STEP, D_MODEL), lambda bi, i: (bi, 0, i, 0)),
                  _layer((1, D_MODEL), layer),
                  _resident((FOURIER_GROUP_DIM, 2 * FOURIER_GROUP_DIM)),
                  _resident((2 * n1, 2 * n1))],
        out_specs=pl.BlockSpec((None, DFT_STEP, n1, D_MODEL), lambda bi, i: (bi, i, 0, 0)),
        compiler_params=_params(2),
        name="dft_stage1",
    )(x.reshape(b, n1, n2, D_MODEL), g, wc, m1)


def _dft2_kernel(y_ref, m2_ref, o_ref):
    for j in range(DFT_STEP):
        rhs = pltpu.bitcast(y_ref[:, j, :], BF16)
        o_ref[:, j, :] = jnp.dot(m2_ref[j], rhs, preferred_element_type=F32)


def _dft2_call(y, m2, s):
    b = y.shape[0]
    n2 = BLOCK
    n1 = s // n2
    blk = pl.BlockSpec((None, n2, DFT_STEP, D_MODEL), lambda bi, i: (bi, 0, i, 0))
    return pl.pallas_call(
        _dft2_kernel,
        out_shape=jax.ShapeDtypeStruct((b, n2, n1, D_MODEL), F32),
        grid=(b, n1 // DFT_STEP),
        in_specs=[blk, pl.BlockSpec((DFT_STEP, n2, 2 * n2), lambda bi, i: (i, 0, 0))],
        out_specs=blk,
        compiler_params=_params(2),
        name="dft_stage2",
    )(y, m2).reshape(b, s, D_MODEL)


def _rope_tables(s):
    half = HEAD_DIM // 2
    inv_freq = ROPE_THETA ** (-jnp.arange(half, dtype=F32) / half)
    ang = jnp.arange(s).astype(F32)[:, None] * inv_freq[None, :]
    cos = jnp.cos(ang)
    sin = jnp.sin(ang)
    return jnp.concatenate([cos, cos], axis=1), jnp.concatenate([-sin, sin], axis=1)


def _band_bias():
    qi = np.arange(BLOCK)[:, None]
    kj = np.arange(3 * BLOCK)[None, :]
    band = np.abs(kj - BLOCK - qi) <= WINDOW
    mid = np.where(band, 0.0, NEG_INF)
    first = np.where(band & (kj >= BLOCK), 0.0, NEG_INF)
    last = np.where(band & (kj < 2 * BLOCK), 0.0, NEG_INF)
    return jnp.asarray(np.stack([mid.T, first.T, last.T]), dtype=F32)


def _segment_ones():
    idx = np.arange(MXU_DIM) // HEAD_DIM
    return jnp.asarray(idx[:, None] == idx[None, :], dtype=BF16)


def _channel_dft():
    c = np.arange(FOURIER_GROUP_DIM)
    ang = 2.0 * np.pi * ((c[:, None] * c[None, :]) % FOURIER_GROUP_DIM) / FOURIER_GROUP_DIM
    return jnp.asarray(np.concatenate([np.cos(ang), -np.sin(ang)], axis=1), dtype=F32).astype(BF16)


def _stage1_dft(n1):
    k = np.arange(n1)
    ang = 2.0 * np.pi * ((k[:, None] * k[None, :]) % n1) / n1
    c, s = np.cos(ang), np.sin(ang)
    m = np.stack([np.concatenate([c, s], axis=1), np.concatenate([-s, c], axis=1)], axis=1)
    return jnp.asarray(m.reshape(2 * n1, 2 * n1), dtype=F32).astype(BF16)


def _stage2_dft(s):
    n2 = BLOCK
    n1 = s // n2
    k1 = np.arange(n1)[:, None]
    k2 = np.arange(n2)[:, None]
    n = np.repeat(np.arange(n2), 2)[None, :]
    quarter = np.tile([0.0, 0.5 * np.pi], n2)[None, :]
    a1 = 2.0 * np.pi * ((k1 * n) % s) / s
    a2 = 2.0 * np.pi * ((k2 * n) % n2) / n2 - quarter
    c1, s1 = jnp.asarray(np.cos(a1), F32)[:, None, :], jnp.asarray(np.sin(a1), F32)[:, None, :]
    c2, s2 = jnp.asarray(np.cos(a2), F32)[None], jnp.asarray(np.sin(a2), F32)[None]
    scale = float(1.0 / np.sqrt(float(s) * FOURIER_GROUP_DIM))
    return ((c1 * c2 - s1 * s2) * scale).astype(BF16)


def _trunk(x, p, tables):
    b, s, _ = x.shape
    cos, sin, bias, seg, wc, m1, m2 = tables
    for i in range(DEPTH):
        j = i // 2
        if i % 2 == 0:
            q, k, vt = _qkv_call(x, j, p["attn_norm_g"], p["w_qkv"], p["q_norm_g"],
                                 p["k_norm_g"], cos, sin, seg)
            mix = _attn_call(p["attn_sinks"], j, q, k, vt, bias)
            x = _proj_ffn_call(x, mix, j, i, p["w_o_attn"],
                               p["ffn_norm_g"], p["w_gate_up"], p["w_down"])
        else:
            y = _dft1_call(x, j, p["fourier_norm_g"], wc, m1)
            mix = _dft2_call(y, m2, s)
            x = _proj_ffn_call(x, mix, j, i, p["w_fourier_out"],
                               p["ffn_norm_g"], p["w_gate_up"], p["w_down"])
    return x


def kernel(x_prompt, x_sample, attn_norm_g, w_qkv, q_norm_g, k_norm_g, attn_sinks, w_o_attn,
           fourier_norm_g, w_fourier_out, ffn_norm_g, w_gate_up, w_down):
    p = {
        "attn_norm_g": attn_norm_g[:, None, :],
        "w_qkv": w_qkv.astype(BF16),
        "q_norm_g": jnp.tile(q_norm_g, (1, N_Q_HEADS))[:, None, :],
        "k_norm_g": jnp.tile(k_norm_g, (1, N_KV_HEADS))[:, None, :],
        "attn_sinks": attn_sinks,
        "w_o_attn": w_o_attn.astype(BF16),
        "fourier_norm_g": fourier_norm_g[:, None, :],
        "w_fourier_out": w_fourier_out.astype(BF16),
        "ffn_norm_g": ffn_norm_g[:, None, :],
        "w_gate_up": w_gate_up.astype(BF16),
        "w_down": w_down.astype(BF16),
    }
    bias, seg, wc = _band_bias(), _segment_ones(), _channel_dft()
    cos, sin = _rope_tables(max(x_prompt.shape[1], x_sample.shape[1]))
    outs = []
    for x in (x_prompt, x_sample):
        s = x.shape[1]
        tables = (cos, sin, bias, seg, wc, _stage1_dft(s // BLOCK), _stage2_dft(s))
        outs.append(_trunk(x, p, tables))
    return tuple(outs)
```
